```python
import jax, jax.numpy as jnp
from jax import lax
import numpy as np

D_MODEL = 2048
BATCH = 32
SEQ = 256
DEPTH = 2
DEC_BATCH = 2
DEC_SEQ = 2048
PAST_LEN = 256

GRID_W = 64
NA_HEADS = 8
NA_HEAD_DIM = 128
NA_WIDTH = NA_HEADS * NA_HEAD_DIM
NA_WIN_H = 8
NA_WIN_W = 16
NA_QCOLS = 8
NA_KCOLS = NA_WIN_W + NA_QCOLS
GLA_HEADS = 4
GLA_DK = 128
GLA_DV = 256
GLA_KW = GLA_HEADS * GLA_DK
GLA_VW = GLA_HEADS * GLA_DV
GLA_LOWRANK = 16
GLA_TAU = 16.0
GLA_CHUNK = 64
ROPE_BASE = 10000.0
CONV_WIDTH = 1024
CONV_TAPS = 31
ALPHA = (2 * DEPTH) ** 0.25
BETA = (8 * DEPTH) ** -0.25
LN_EPS = 1e-5
NEG_INF = -1e30
SPLIT_SIZES = (NA_WIDTH, NA_WIDTH, NA_WIDTH, NA_WIDTH,
               GLA_KW, GLA_KW, GLA_VW, 2 * GLA_LOWRANK, GLA_VW,
               CONV_WIDTH, CONV_WIDTH, CONV_WIDTH,
               D_MODEL, D_MODEL, D_MODEL)
N_IN = sum(SPLIT_SIZES)

kernel_name = 'hybrid_natten_gla_conformer_dit_step'


def layer_norm(x, g=None, b=None):
    xf = x.astype(jnp.float32)
    mu = jnp.mean(xf, -1, keepdims=True)
    var = jnp.mean(jnp.square(xf - mu), -1, keepdims=True)
    y = (xf - mu) * lax.rsqrt(var + LN_EPS)
    if g is not None:
        y = y * g + b
    return y.astype(x.dtype)


def split_cols(p):
    offs = np.cumsum(SPLIT_SIZES)[:-1].tolist()
    return jnp.split(p, offs, axis=-1)


def axial_rope(x):
    seq_len, dk = x.shape[1], x.shape[-1]
    t = jnp.arange(seq_len)
    half = dk // 2
    inv = ROPE_BASE ** (-jnp.arange(0, half, 2, dtype=jnp.float32) / half)

    def rot(xa, pos):
        ang = pos.astype(jnp.float32)[:, None] * inv[None, :]
        cos = jnp.cos(ang)[None, :, None, :]
        sin = jnp.sin(ang)[None, :, None, :]
        x1, x2 = xa[..., :half // 2], xa[..., half // 2:]
        return jnp.concatenate([x1 * cos - x2 * sin, x1 * sin + x2 * cos], -1)

    out = jnp.concatenate([rot(x[..., :half], t // GRID_W), rot(x[..., half:], t % GRID_W)], -1)
    return out.astype(x.dtype)


def context_attention(q, k, v):
    bsz, seq_len, nh, hd = q.shape
    s = jnp.einsum('bqhd,bkhd->bhqk', q, k, preferred_element_type=jnp.float32) * hd ** -0.5
    p = jax.nn.softmax(s, axis=-1).astype(v.dtype)
    return jnp.einsum('bhqk,bkhd->bqhd', p, v).reshape(bsz, seq_len, nh * hd)


def na_index_tables(rows):
    kh = min(NA_WIN_H, rows)
    ncb = GRID_W // NA_QCOLS
    r = np.arange(rows)
    rs = np.clip(r - kh // 2, 0, rows - kh)
    c = np.arange(GRID_W)
    cs = np.clip(c - NA_WIN_W // 2, 0, GRID_W - NA_WIN_W)
    j = np.arange(ncb)
    bs = np.clip(j * NA_QCOLS - NA_WIN_W // 2, 0, GRID_W - NA_KCOLS)
    key_row = rs[:, None] + np.arange(kh)[None, :]
    key_col = bs[:, None] + np.arange(NA_KCOLS)[None, :]
    key_idx = key_row[:, None, :, None] * GRID_W + key_col[None, :, None, :]
    qcol = j[:, None] * NA_QCOLS + np.arange(NA_QCOLS)[None, :]
    dr = key_row - r[:, None]
    dc = key_col[:, None, :] - qcol[:, :, None]
    start = cs[qcol][:, :, None]
    valid = (key_col[:, None, :] >= start) & (key_col[:, None, :] < start + NA_WIN_W)
    full = (rows, ncb, NA_QCOLS, kh, NA_KCOLS)
    kk = kh * NA_KCOLS
    dr_idx = np.broadcast_to(dr[:, None, None, :, None], full).reshape(rows, ncb, NA_QCOLS, kk) + (NA_WIN_H - 1)
    dc_idx = np.clip(np.broadcast_to(dc[None, :, :, None, :], full).reshape(rows, ncb, NA_QCOLS, kk) + (NA_WIN_W - 1), 0, 2 * NA_WIN_W - 2)
    valid = np.broadcast_to(valid[None, :, :, None, :], full).reshape(rows, ncb, NA_QCOLS, kk)
    return key_idx.reshape(rows, ncb, kk), dr_idx, dc_idx, valid


def neighbourhood_attention(q, k, v, k_ctx, v_ctx, rpb):
    bsz, n_tok, nh, hd = q.shape
    rows = n_tok // GRID_W
    ncb = GRID_W // NA_QCOLS
    key_idx, dr_idx, dc_idx, valid = na_index_tables(rows)
    bias = jnp.where(valid, rpb[:, dr_idx, dc_idx].astype(jnp.float32), NEG_INF)
    bias = jnp.moveaxis(bias, 1, 0)
    qb = jnp.moveaxis(q.reshape(bsz, rows, ncb, NA_QCOLS, nh, hd), 1, 0)
    scale = hd ** -0.5
    n_win = key_idx.shape[-1]

    def row_block(inp):
        q_r, idx_r, bias_r = inp
        k_r = jnp.take(k, idx_r, axis=1)
        v_r = jnp.take(v, idx_r, axis=1)
        s_win = jnp.einsum('bnqhd,bnkhd->bhnqk', q_r, k_r, preferred_element_type=jnp.float32) * scale + bias_r
        s_ctx = jnp.einsum('bnqhd,bphd->bhnqp', q_r, k_ctx, preferred_element_type=jnp.float32) * scale
        p = jax.nn.softmax(jnp.concatenate([s_win, s_ctx], -1), axis=-1)
        return (jnp.einsum('bhnqk,bnkhd->bnqhd', p[..., :n_win].astype(v.dtype), v_r)
                + jnp.einsum('bhnqp,bphd->bnqhd', p[..., n_win:].astype(v.dtype), v_ctx))

    o = lax.map(row_block, (qb, jnp.asarray(key_idx, jnp.int32), bias))
    return jnp.moveaxis(o, 0, 1).reshape(bsz, n_tok, nh * hd)


def gla_chunk_scan(q, k, v, log_a, s0):
    bsz, seq_len, nh, dk = q.shape
    dv = v.shape[-1]
    n_chunks = seq_len // GLA_CHUNK
    chunk = lambda t: t.reshape(bsz, n_chunks, GLA_CHUNK, nh, t.shape[-1])
    q, k, v, log_a = chunk(q), chunk(k), chunk(v), chunk(log_a)
    b = jnp.cumsum(log_a, axis=2)
    b_last = b[:, :, -1:]
    q_e = q * jnp.exp(b)
    k_e = k * jnp.exp(-b)
    causal = jnp.tril(jnp.ones((GLA_CHUNK, GLA_CHUNK), bool))
    att = jnp.where(causal, jnp.einsum('bnihd,bnjhd->bnhij', q_e, k_e), 0.0)
    o_intra = jnp.einsum('bnhij,bnjhe->bnihe', att, v)
    kv = jnp.einsum('bnjhd,bnjhe->bnhde', k * jnp.exp(b_last - b), v)
    decay = jnp.exp(b_last[:, :, 0])

    def step(s, inp):
        dec, kv_n = inp
        return dec[..., None] * s + kv_n, s

    s_final, s_before = lax.scan(step, s0, (jnp.moveaxis(decay, 1, 0), jnp.moveaxis(kv, 1, 0)))
    o_inter = jnp.einsum('bnihd,bnhde->bnihe', q_e, jnp.moveaxis(s_before, 0, 1))
    return (o_intra + o_inter).reshape(bsz, seq_len, nh, dv), s_final


def gla_bidir(q, k, v, la_f, la_b, s0_f, s0_b):
    o_f, s_f = gla_chunk_scan(q, k, v, la_f, s0_f)
    flip = lambda t: t[:, ::-1]
    o_b, s_b = gla_chunk_scan(flip(q), flip(k), flip(v), flip(la_b), s0_b)
    diag = jnp.einsum('blhd,blhd->blh', q, k)[..., None] * v
    return o_f + flip(o_b) - diag, s_f, s_b


def gla_branch(b_q, b_k, b_v, b_lr, w_gate, b_gate, norm_g, s0, latent):
    bsz, seq_len = b_q.shape[:2]
    hs = lambda t, d: t.reshape(bsz, seq_len, GLA_HEADS, d).astype(jnp.float32)
    q = hs(b_q, GLA_DK) * GLA_DK ** -0.5
    k = hs(b_k, GLA_DK)
    v = hs(b_v, GLA_DV)
    if latent:
        q, k = axial_rope(q), axial_rope(k)
    lr = b_lr.astype(jnp.float32).reshape(bsz, seq_len, 2, GLA_LOWRANK)
    gate_pre = jnp.einsum('blzr,zrk->blzk', lr, w_gate.astype(jnp.float32)) + b_gate.astype(jnp.float32)
    la = (jax.nn.log_sigmoid(gate_pre) / GLA_TAU).reshape(bsz, seq_len, 2, GLA_HEADS, GLA_DK)
    o, s_f, s_b = gla_bidir(q, k, v, la[:, :, 0], la[:, :, 1], s0[:, 0], s0[:, 1])
    o = o * lax.rsqrt(jnp.mean(o * o, -1, keepdims=True) + LN_EPS) * norm_g.astype(jnp.float32).reshape(GLA_HEADS, GLA_DV)
    return o.reshape(bsz, seq_len, GLA_VW), jnp.stack([s_f, s_b], axis=1)


def conformer_conv(c_u, c_g, w_dw, b_dw, g_n, b_n):
    u = c_u * jax.nn.sigmoid(c_g)
    u = lax.conv_general_dilated(u, w_dw[:, None, :].astype(u.dtype), window_strides=(1,),
                                 padding=[(CONV_TAPS // 2, CONV_TAPS // 2)],
                                 dimension_numbers=('NWC', 'WIO', 'NWC'),
                                 feature_group_count=CONV_WIDTH) + b_dw
    return jax.nn.silu(layer_norm(u, g_n, b_n))


def trunk_layer(x, mod, lw, latent, k_ctx=None, v_ctx=None, s_ctx=None):
    bsz, seq_len, _ = x.shape
    shift, scale, gate = jnp.split(mod, 3, axis=-1)
    h = layer_norm(x) * (1 + scale) + shift
    (a_q, a_k, a_v, a_z, b_q, b_k, b_v, b_lr, b_z,
     c_u, c_g, c_z, g_a, g_b, g_c) = split_cols(h @ lw['w_in'])
    heads = lambda t: t.reshape(bsz, seq_len, NA_HEADS, NA_HEAD_DIM)
    a_q, a_k, a_v = heads(a_q), heads(a_k), heads(a_v)
    if latent:
        o_a = neighbourhood_attention(a_q, a_k, a_v, k_ctx, v_ctx, lw['rpb'])
        s0 = s_ctx.astype(jnp.float32)
    else:
        o_a = context_attention(a_q, a_k, a_v)
        s0 = jnp.zeros((bsz, 2, GLA_HEADS, GLA_DK, GLA_DV), jnp.float32)
    o_b, s_fin = gla_branch(b_q, b_k, b_v, b_lr, lw['gla_w_gate'], lw['gla_b_gate'], lw['gla_norm_g'], s0, latent)
    o_c = conformer_conv(c_u, c_g, lw['conv_w'], lw['conv_b'], lw['conv_norm_g'], lw['conv_norm_b'])
    dt = x.dtype
    br_a = (o_a * jax.nn.silu(a_z)).astype(dt) @ lw['w_proj_a']
    br_b = (o_b * jax.nn.silu(b_z)).astype(dt) @ lw['w_proj_b']
    br_c = (o_c * jax.nn.silu(c_z)).astype(dt) @ lw['w_proj_c']
    merged = jax.nn.sigmoid(g_a) * br_a + jax.nn.sigmoid(g_b) * br_b + jax.nn.sigmoid(g_c) * br_c
    y = layer_norm(ALPHA * x + gate * (merged @ lw['w_out']), lw['ln_g'], lw['ln_b'])
    return y, a_k, a_v, s_fin.astype(dt)


def setup_inputs(seed: int = 0) -> dict:
    key = jax.random.key(seed)
    ks = jax.random.split(key, 26)
    nrm = lambda k, shape, s: jax.random.normal(k, shape, jnp.float32) * s
    return {
        'x_prompt': nrm(ks[0], (BATCH, SEQ, D_MODEL), 1.0),
        'x_sample': nrm(ks[1], (DEC_BATCH, DEC_SEQ, D_MODEL), 1.0),
        'cache_k': nrm(ks[2], (DEC_BATCH, DEPTH, PAST_LEN, NA_HEADS, NA_HEAD_DIM), 1.0),
        'cache_v': nrm(ks[3], (DEC_BATCH, DEPTH, PAST_LEN, NA_HEADS, NA_HEAD_DIM), 1.0),
        'state_gla': nrm(ks[4], (DEC_BATCH, DEPTH, 2, GLA_HEADS, GLA_DK, GLA_DV), 2.0),
        'c': nrm(ks[5], (DEC_BATCH, D_MODEL), 1.0),
        'c_ctx': nrm(ks[6], (D_MODEL,), 1.0),
        'w_mod': nrm(ks[7], (DEPTH, D_MODEL, 3 * D_MODEL), 0.5 * D_MODEL ** -0.5),
        'b_mod': nrm(ks[8], (DEPTH, 3 * D_MODEL), 0.01),
        'w_in': nrm(ks[9], (DEPTH, D_MODEL, N_IN), D_MODEL ** -0.5),
        'rpb': nrm(ks[10], (DEPTH, NA_HEADS, 2 * NA_WIN_H - 1, 2 * NA_WIN_W - 1), 0.1),
        'gla_w_gate': nrm(ks[11], (DEPTH, 2, GLA_LOWRANK, GLA_KW), GLA_LOWRANK ** -0.5),
        'gla_b_gate': nrm(ks[12], (DEPTH, 2, GLA_KW), 0.01),
        'gla_norm_g': 1.0 + nrm(ks[13], (DEPTH, GLA_VW), 0.01),
        'conv_w': nrm(ks[14], (DEPTH, CONV_TAPS, CONV_WIDTH), CONV_TAPS ** -0.5),
        'conv_b': nrm(ks[15], (DEPTH, CONV_WIDTH), 0.01),
        'conv_norm_g': 1.0 + nrm(ks[16], (DEPTH, CONV_WIDTH), 0.01),
        'conv_norm_b': nrm(ks[17], (DEPTH, CONV_WIDTH), 0.01),
        'w_proj_a': nrm(ks[18], (DEPTH, NA_WIDTH, D_MODEL), BETA * NA_WIDTH ** -0.5),
        'w_proj_b': nrm(ks[19], (DEPTH, GLA_VW, D_MODEL), BETA * GLA_VW ** -0.5),
        'w_proj_c': nrm(ks[20], (DEPTH, CONV_WIDTH, D_MODEL), BETA * CONV_WIDTH ** -0.5),
        'w_out': nrm(ks[21], (DEPTH, D_MODEL, D_MODEL), BETA * D_MODEL ** -0.5),
        'ln_g': 1.0 + nrm(ks[22], (DEPTH, D_MODEL), 0.01),
        'ln_b': nrm(ks[23], (DEPTH, D_MODEL), 0.01),
    }


def reference(x_prompt, x_sample, cache_k, cache_v, state_gla, c, c_ctx, w_mod, b_mod, w_in, rpb,
              gla_w_gate, gla_b_gate, gla_norm_g, conv_w, conv_b, conv_norm_g, conv_norm_b,
              w_proj_a, w_proj_b, w_proj_c, w_out, ln_g, ln_b):
    h_ctx = x_prompt
    h_lat = x_sample
    ks_list, vs_list, ss_list = [], [], []
    for l in range(DEPTH):
        lw = dict(w_in=w_in[l], rpb=rpb[l], gla_w_gate=gla_w_gate[l], gla_b_gate=gla_b_gate[l],
                  gla_norm_g=gla_norm_g[l], conv_w=conv_w[l], conv_b=conv_b[l],
                  conv_norm_g=conv_norm_g[l], conv_norm_b=conv_norm_b[l], w_proj_a=w_proj_a[l],
                  w_proj_b=w_proj_b[l], w_proj_c=w_proj_c[l], w_out=w_out[l], ln_g=ln_g[l], ln_b=ln_b[l])
        mod_ctx = (jax.nn.silu(c_ctx) @ w_mod[l] + b_mod[l])[None, None, :]
        h_ctx, k_l, v_l, s_l = trunk_layer(h_ctx, mod_ctx, lw, False)
        ks_list.append(k_l)
        vs_list.append(v_l)
        ss_list.append(s_l)
        mod_lat = (jax.nn.silu(c) @ w_mod[l] + b_mod[l])[:, None, :]
        h_lat = trunk_layer(h_lat, mod_lat, lw, True, cache_k[:, l], cache_v[:, l], state_gla[:, l])[0]
    new_k = jnp.stack(ks_list, axis=1)
    new_v = jnp.stack(vs_list, axis=1)
    new_state_gla = jnp.stack(ss_list, axis=1)
    return (h_ctx, h_lat, new_k, new_v, new_state_gla)
```

```python
import functools

import numpy as np
import jax
import jax.numpy as jnp
from jax import lax
from jax.experimental import pallas as pl
from jax.experimental.pallas import tpu as pltpu

D_MODEL = 2048
DEPTH = 2
GRID_W = 64
NA_HEADS = 8
NA_HEAD_DIM = 128
NA_WIDTH = NA_HEADS * NA_HEAD_DIM
NA_WIN_H = 8
NA_WIN_W = 16
GLA_HEADS = 4
GLA_DK = 128
GLA_DV = 256
GLA_KW = GLA_HEADS * GLA_DK
GLA_VW = GLA_HEADS * GLA_DV
GLA_LOWRANK = 16
GLA_TAU = 16.0
GLA_CHUNK = 64
ROPE_BASE = 10000.0
CONV_WIDTH = 1024
CONV_TAPS = 31
ALPHA = (2 * DEPTH) ** 0.25
LN_EPS = 1e-5
NEG_INF = -1e30

LANES = 128
CONV_HALO = 16
VMEM_LIMIT = 56 * 1024 * 1024

COL_AQ, COL_AK, COL_AV, COL_AZ = 0, 1024, 2048, 3072
COL_BQ, COL_BK, COL_BV, COL_BZ = 4096, 4608, 5120, 6144
COL_CU, COL_CG, COL_CZ = 7168, 8192, 9216
COL_GA, COL_GB, COL_GC = 10240, 12288, 14336
N_MAIN = 16384
LR_START = 6144

BF16 = jnp.bfloat16
F32 = jnp.float32


def _cparams(*sem):
    return pltpu.CompilerParams(dimension_semantics=sem, vmem_limit_bytes=VMEM_LIMIT)


def _silu(x):
    return x * jax.nn.sigmoid(x)


def _dot(a, b):
    return jnp.dot(a, b, preferred_element_type=F32)


def _dot_nt(a, b):
    return lax.dot_general(a, b, (((1,), (1,)), ((), ())), preferred_element_type=F32)


def _mod_kernel(cv_ref, w_ref, b_ref, o_ref):
    s = _silu(cv_ref[...]).astype(BF16)
    o_ref[0] = _dot(s, w_ref[0].astype(BF16)) + b_ref[0]


def _modulation(cvec, w_mod, b_mod):
    tn = 1024
    n3 = 3 * D_MODEL
    return pl.pallas_call(
        _mod_kernel,
        grid=(DEPTH, n3 // tn),
        in_specs=[pl.BlockSpec((8, D_MODEL), lambda l, n: (0, 0)),
                  pl.BlockSpec((1, D_MODEL, tn), lambda l, n: (l, 0, n)),
                  pl.BlockSpec((1, 1, tn), lambda l, n: (l, 0, n))],
        out_specs=pl.BlockSpec((1, 8, tn), lambda l, n: (l, 0, n)),
        out_shape=jax.ShapeDtypeStruct((DEPTH, 8, n3), F32),
        compiler_params=_cparams("parallel", "parallel"),
        name="modulation",
    )(cvec, w_mod, b_mod.reshape(DEPTH, 1, n3))


def _inproj_kernel(x_ref, mod_ref, w_ref, wlr_ref, o_ref, olr_ref, h_scr):
    strip = 256

    @pl.when(pl.program_id(1) == 0)
    def _():
        shift = mod_ref[0, :, 0:D_MODEL]
        scale = mod_ref[0, :, D_MODEL:2 * D_MODEL]

        def body(i, carry):
            sl = pl.ds(pl.multiple_of(i * strip, strip), strip)
            x = x_ref[sl, :]
            mu = jnp.mean(x, -1, keepdims=True)
            xc = x - mu
            var = jnp.mean(xc * xc, -1, keepdims=True)
            y = xc * lax.rsqrt(var + LN_EPS)
            hb = (y * (1.0 + scale) + shift).astype(BF16)
            h_scr[sl, :] = hb
            olr_ref[sl, :] = _dot(hb, wlr_ref[...])
            return carry

        lax.fori_loop(0, x_ref.shape[0] // strip, body, 0)

    o_ref[...] = _dot(h_scr[...], w_ref[...])


def _in_projection(x2d, mod, w_main, w_lr, seq_len):
    t_tok = x2d.shape[0]
    tm, tn = 1024, 1024
    per = seq_len // tm if mod.shape[0] > 1 else t_tok // tm
    return pl.pallas_call(
        _inproj_kernel,
        grid=(t_tok // tm, N_MAIN // tn),
        in_specs=[pl.BlockSpec((tm, D_MODEL), lambda m, n: (m, 0)),
                  pl.BlockSpec((1, 1, 3 * D_MODEL), lambda m, n: (m // per, 0, 0)),
                  pl.BlockSpec((D_MODEL, tn), lambda m, n: (0, n)),
                  pl.BlockSpec((D_MODEL, LANES), lambda m, n: (0, 0))],
        out_specs=[pl.BlockSpec((tm, tn), lambda m, n: (m, n)),
                   pl.BlockSpec((tm, LANES), lambda m, n: (m, 0))],
        out_shape=[jax.ShapeDtypeStruct((t_tok, N_MAIN), F32),
                   jax.ShapeDtypeStruct((t_tok, LANES), F32)],
        scratch_shapes=[pltpu.VMEM((tm, D_MODEL), BF16)],
        compiler_params=_cparams("parallel", "arbitrary"),
        name="in_projection",
    )(x2d, mod, w_main, w_lr)


def _ctx_attn_kernel(q_ref, k_ref, v_ref, z_ref, o_ref):
    scale = NA_HEAD_DIM ** -0.5
    for h in range(NA_HEADS):
        cs = slice(h * NA_HEAD_DIM, (h + 1) * NA_HEAD_DIM)
        q = q_ref[:, cs].astype(BF16)
        k = k_ref[:, cs].astype(BF16)
        v = v_ref[:, cs].astype(BF16)
        s = _dot_nt(q, k) * scale
        e = jnp.exp(s - jnp.max(s, -1, keepdims=True))
        p = e / jnp.sum(e, -1, keepdims=True)
        o = _dot(p.astype(BF16), v)
        o_ref[:, cs] = (o * _silu(z_ref[:, cs])).astype(BF16)


def _context_attention(p_ctx, bsz, seq_len):
    blk = lambda j: pl.BlockSpec((seq_len, NA_WIDTH), lambda b: (b, j))
    return pl.pallas_call(
        _ctx_attn_kernel,
        grid=(bsz,),
        in_specs=[blk(COL_AQ // NA_WIDTH), blk(COL_AK // NA_WIDTH), blk(COL_AV // NA_WIDTH), blk(COL_AZ // NA_WIDTH)],
        out_specs=pl.BlockSpec((seq_len, NA_WIDTH), lambda b: (b, 0)),
        out_shape=jax.ShapeDtypeStruct((bsz * seq_len, NA_WIDTH), BF16),
        compiler_params=_cparams("parallel"),
        name="context_attention",
    )(p_ctx, p_ctx, p_ctx, p_ctx)


def _na_bias_table(rpb_l, rows):
    kh = min(NA_WIN_H, rows)
    c = np.arange(GRID_W)
    cs = np.clip(c - NA_WIN_W // 2, 0, GRID_W - NA_WIN_W)
    kc = np.arange(GRID_W)
    valid = (kc[None, :] >= cs[:, None]) & (kc[None, :] < cs[:, None] + NA_WIN_W)
    dc_idx = np.clip(kc[None, :] - c[:, None] + (NA_WIN_W - 1), 0, 2 * NA_WIN_W - 2)
    off = np.arange(NA_WIN_H)[:, None]
    dr_idx = np.clip(off + np.arange(kh)[None, :], 0, 2 * NA_WIN_H - 2)
    tab = rpb_l.astype(F32)[:, dr_idx[:, :, None, None], dc_idx[None, None, :, :]]
    tab = jnp.where(valid[None, None, None], tab, NEG_INF)
    tab = jnp.moveaxis(tab, 2, 3)
    return tab.reshape(NA_HEADS, NA_WIN_H, GRID_W, kh * GRID_W)


def _na_kernel(q_ref, k_ref, v_ref, z_ref, kc_ref, vc_ref, bias_ref, o_ref, kb_scr, vb_scr, *, rows, kh):
    scale = NA_HEAD_DIM ** -0.5
    kb_scr[...] = k_ref[...].astype(BF16)
    vb_scr[...] = v_ref[...].astype(BF16)
    kctx = kc_ref[...].astype(BF16)
    vctx = vc_ref[...].astype(BF16)
    nwin = kh * GRID_W

    def row(r, carry):
        rs = jnp.clip(r - kh // 2, 0, rows - kh)
        off = rs - r + (NA_WIN_H - 1)
        qsl = pl.ds(pl.multiple_of(r * GRID_W, GRID_W), GRID_W)
        ksl = pl.ds(pl.multiple_of(rs * GRID_W, GRID_W), nwin)
        q = q_ref[qsl, :].astype(BF16)
        s_win = _dot_nt(q, kb_scr[ksl, :]) * scale + bias_ref[off]
        s_ctx = _dot_nt(q, kctx) * scale
        m = jnp.maximum(jnp.max(s_win, -1, keepdims=True), jnp.max(s_ctx, -1, keepdims=True))
        e_win = jnp.exp(s_win - m)
        e_ctx = jnp.exp(s_ctx - m)
        den = jnp.sum(e_win, -1, keepdims=True) + jnp.sum(e_ctx, -1, keepdims=True)
        o = _dot((e_win / den).astype(BF16), vb_scr[ksl, :]) + _dot((e_ctx / den).astype(BF16), vctx)
        o_ref[qsl, :] = (o * _silu(z_ref[qsl, :])).astype(BF16)
        return carry

    lax.fori_loop(0, rows, row, 0)


def _neighbourhood_attention(p_lat, cache_k, cache_v, bias_tab, layer, bsz, n_tok):
    rows = n_tok // GRID_W
    kh = min(NA_WIN_H, rows)
    past = cache_k.shape[2]
    hd = NA_HEAD_DIM
    blk = lambda col: pl.BlockSpec((n_tok, hd), lambda b, h: (b, col // hd + h))
    cblk = pl.BlockSpec((None, None, past, hd), lambda b, h: (b, layer, 0, h))
    return pl.pallas_call(
        functools.partial(_na_kernel, rows=rows, kh=kh),
        grid=(bsz, NA_HEADS),
        in_specs=[blk(COL_AQ), blk(COL_AK), blk(COL_AV), blk(COL_AZ), cblk, cblk,
                  pl.BlockSpec((None, NA_WIN_H, GRID_W, kh * GRID_W), lambda b, h: (h, 0, 0, 0))],
        out_specs=pl.BlockSpec((n_tok, hd), lambda b, h: (b, h)),
        out_shape=jax.ShapeDtypeStruct((bsz * n_tok, NA_WIDTH), BF16),
        scratch_shapes=[pltpu.VMEM((n_tok, hd), BF16), pltpu.VMEM((n_tok, hd), BF16)],
        compiler_params=_cparams("parallel", "parallel"),
        name="neighbourhood_attention",
    )(p_lat, p_lat, p_lat, p_lat, cache_k, cache_v, bias_tab)


def _rope_tables(seq_len):
    t = jnp.arange(seq_len)
    half = GLA_DK // 2
    inv = ROPE_BASE ** (-jnp.arange(0, half, 2, dtype=F32) / half)

    def tab(pos):
        ang = pos.astype(F32)[:, None] * inv[None, :]
        return jnp.cos(ang), jnp.sin(ang)

    cr, sr = tab(t // GRID_W)
    cc, sc = tab(t % GRID_W)
    cos = jnp.concatenate([cr, cr, cc, cc], -1)
    sin = jnp.concatenate([-sr, sr, -sc, sc], -1)
    return cos, sin


def _rope(x, cos, sin):
    quarter = GLA_DK // 4
    lane = lax.broadcasted_iota(jnp.int32, x.shape, 1)
    partner = jnp.where((lane % (2 * quarter)) < quarter,
                        pltpu.roll(x, GLA_DK - quarter, 1), pltpu.roll(x, quarter, 1))
    return x * cos + partner * sin


def _log_sigmoid(x):
    return jnp.minimum(x, 0.0) - jnp.log1p(jnp.exp(-jnp.abs(x)))


def _split3(x):
    hi = x.astype(BF16)
    r1 = x - hi.astype(F32)
    mid = r1.astype(BF16)
    lo = (r1 - mid.astype(F32)).astype(BF16)
    return hi, mid, lo


def _tri_cumsum(tri, x):
    hi, mid, lo = _split3(x)
    return _dot(tri, hi) + _dot(tri, mid) + _dot(tri, lo)


def _gla_kernel(*refs, latent, n_chunks):
    (q_ref, k_ref, v_ref, z_ref, lr_ref, wgf_ref, wgb_ref, bg_ref, ng_ref), rest = refs[:9], refs[9:]
    if latent:
        cos_ref, sin_ref, s0_ref, o_ref, qs, ks, laf, lab, oacc, st = rest
    else:
        o_ref, sfin_ref, qs, ks, laf, lab, oacc, st = rest
    c_len = GLA_CHUNK

    q = q_ref[...] * (GLA_DK ** -0.5)
    k = k_ref[...]
    if latent:
        q = _rope(q, cos_ref[...], sin_ref[...])
        k = _rope(k, cos_ref[...], sin_ref[...])
    qs[...] = q
    ks[...] = k
    lrb = lr_ref[...].astype(BF16)
    laf[...] = _log_sigmoid(_dot(lrb, wgf_ref[...]) + bg_ref[0:1, :]) * (1.0 / GLA_TAU)
    lab[...] = _log_sigmoid(_dot(lrb, wgb_ref[...]) + bg_ref[1:2, :]) * (1.0 / GLA_TAU)

    ii = lax.broadcasted_iota(jnp.int32, (c_len, c_len), 0)
    jj = lax.broadcasted_iota(jnp.int32, (c_len, c_len), 1)
    lower = ii >= jj
    tri_lo = jnp.where(lower, 1.0, 0.0).astype(BF16)
    tri_up = jnp.where(ii <= jj, 1.0, 0.0).astype(BF16)

    def chunk(c, la_ref, tri, mask, edge, accumulate):
        sl = pl.ds(pl.multiple_of(c * c_len, c_len), c_len)
        b = _tri_cumsum(tri, la_ref[sl, :])
        btot = b[edge:edge + 1, :]
        qc = qs[sl, :]
        kc = ks[sl, :]
        vc = v_ref[sl, :]
        qe = (qc * jnp.exp(b)).astype(BF16)
        ke = (kc * jnp.exp(-b)).astype(BF16)
        kw = (kc * jnp.exp(btot - b)).astype(BF16)
        vb = vc.astype(BF16)
        att = jnp.where(mask, _dot_nt(qe, ke), 0.0)
        s_t = st[...]
        o = _dot(att.astype(BF16), vb) + _dot_nt(qe, s_t.astype(BF16))
        if accumulate:
            oacc[sl, :] = oacc[sl, :] + o
        else:
            oacc[sl, :] = o
        st[...] = jnp.exp(btot) * s_t + _dot(vc.T.astype(BF16), kw)

    def run(direction):
        fwd = direction == 0
        if latent:
            st[...] = s0_ref[direction].T
        else:
            st[...] = jnp.zeros_like(st)
        args = (laf, tri_lo, lower, c_len - 1, False) if fwd else (lab, tri_up, ii <= jj, 0, True)
        if n_chunks <= 8:
            order = range(n_chunks) if fwd else range(n_chunks - 1, -1, -1)
            for c in order:
                chunk(c, *args)
        else:
            def body(i, carry):
                chunk(i if fwd else n_chunks - 1 - i, *args)
                return carry
            lax.fori_loop(0, n_chunks, body, 0)
        if not latent:
            sfin_ref[direction] = st[...].T

    run(0)
    run(1)

    v = v_ref[...]
    o = oacc[...] - jnp.sum(qs[...] * ks[...], -1, keepdims=True) * v
    o = o * lax.rsqrt(jnp.mean(o * o, -1, keepdims=True) + LN_EPS) * ng_ref[...]
    o_ref[...] = (o * _silu(z_ref[...])).astype(BF16)


def _gla(p, p_lr, wg_f, wg_b, b_gate, norm_g, bsz, seq_len, rope=None, state=None, layer=0):
    latent = state is not None
    dk, dv = GLA_DK, GLA_DV
    in_specs = [pl.BlockSpec((seq_len, dk), lambda b, h: (b, COL_BQ // dk + h)),
                pl.BlockSpec((seq_len, dk), lambda b, h: (b, COL_BK // dk + h)),
                pl.BlockSpec((seq_len, dv), lambda b, h: (b, COL_BV // dv + h)),
                pl.BlockSpec((seq_len, dv), lambda b, h: (b, COL_BZ // dv + h)),
                pl.BlockSpec((seq_len, LANES), lambda b, h: (b, 0)),
                pl.BlockSpec((LANES, dk), lambda b, h: (0, h)),
                pl.BlockSpec((LANES, dk), lambda b, h: (0, h)),
                pl.BlockSpec((2, dk), lambda b, h: (0, h)),
                pl.BlockSpec((1, dv), lambda b, h: (0, h))]
    args = [p, p, p, p, p_lr, wg_f, wg_b, b_gate, norm_g]
    o_spec = pl.BlockSpec((seq_len, dv), lambda b, h: (b, h))
    o_shape = jax.ShapeDtypeStruct((bsz * seq_len, GLA_VW), BF16)
    if latent:
        cos, sin = rope
        in_specs += [pl.BlockSpec((seq_len, dk), lambda b, h: (0, 0)),
                     pl.BlockSpec((seq_len, dk), lambda b, h: (0, 0)),
                     pl.BlockSpec((None, None, 2, None, dk, dv), lambda b, h: (b, layer, 0, h, 0, 0))]
        args += [cos, sin, state]
        out_specs, out_shape = o_spec, o_shape
    else:
        out_specs = [o_spec, pl.BlockSpec((None, 2, None, dk, dv), lambda b, h: (b, 0, h, 0, 0))]
        out_shape = [o_shape, jax.ShapeDtypeStruct((bsz, 2, GLA_HEADS, dk, dv), F32)]
    return pl.pallas_call(
        functools.partial(_gla_kernel, latent=latent, n_chunks=seq_len // GLA_CHUNK),
        grid=(bsz, GLA_HEADS),
        in_specs=in_specs,
        out_specs=out_specs,
        out_shape=out_shape,
        scratch_shapes=[pltpu.VMEM((seq_len, dk), F32), pltpu.VMEM((seq_len, dk), F32),
                        pltpu.VMEM((seq_len, dk), F32), pltpu.VMEM((seq_len, dk), F32),
                        pltpu.VMEM((seq_len, dv), F32), pltpu.VMEM((dv, dk), F32)],
        compiler_params=_cparams("parallel", "parallel"),
        name="gla_latent" if latent else "gla_context",
    )(*args)


def _conv_kernel(u_ref, g_ref, up_ref, gp_ref, un_ref, gn_ref, z_ref, w_ref, b_ref, ng_ref, nb_ref, o_ref, pad,
                 *, tiles_per_seq, tt):
    t = pl.program_id(0) % tiles_per_seq
    halo = CONV_HALO
    glu = lambda u, g: u * jax.nn.sigmoid(g)
    pad[halo:halo + tt, :] = glu(u_ref[...], g_ref[...])
    pad[0:halo, :] = jnp.where(t > 0, glu(up_ref[...], gp_ref[...]), 0.0)
    pad[halo + tt:halo + tt + halo, :] = jnp.where(t < tiles_per_seq - 1, glu(un_ref[...], gn_ref[...]), 0.0)
    acc = jnp.broadcast_to(b_ref[...], (tt, CONV_WIDTH))
    first = halo - CONV_TAPS // 2
    for j in range(CONV_TAPS):
        acc = acc + pad[first + j:first + j + tt, :] * w_ref[j:j + 1, :]
    mu = jnp.mean(acc, -1, keepdims=True)
    xc = acc - mu
    var = jnp.mean(xc * xc, -1, keepdims=True)
    y = xc * lax.rsqrt(var + LN_EPS) * ng_ref[...] + nb_ref[...]
    o_ref[...] = (_silu(y) * _silu(z_ref[...])).astype(BF16)


def _conformer_conv(p, conv_w, conv_b, norm_g, norm_b, seq_len):
    t_tok = p.shape[0]
    tt = 256
    tiles_per_seq = seq_len // tt
    w = CONV_WIDTH
    hb = tt // CONV_HALO
    n_halo = t_tok // CONV_HALO
    main = lambda col: pl.BlockSpec((tt, w), lambda i: (i, col // w))
    prev = lambda col: pl.BlockSpec((CONV_HALO, w), lambda i: (jnp.maximum(i * hb - 1, 0), col // w))
    nxt = lambda col: pl.BlockSpec((CONV_HALO, w), lambda i: (jnp.minimum((i + 1) * hb, n_halo - 1), col // w))
    vec = pl.BlockSpec((1, w), lambda i: (0, 0))
    return pl.pallas_call(
        functools.partial(_conv_kernel, tiles_per_seq=tiles_per_seq, tt=tt),
        grid=(t_tok // tt,),
        in_specs=[main(COL_CU), main(COL_CG), prev(COL_CU), prev(COL_CG), nxt(COL_CU), nxt(COL_CG), main(COL_CZ),
                  pl.BlockSpec((CONV_TAPS, w), lambda i: (0, 0)), vec, vec, vec],
        out_specs=pl.BlockSpec((tt, w), lambda i: (i, 0)),
        out_shape=jax.ShapeDtypeStruct((t_tok, w), BF16),
        scratch_shapes=[pltpu.VMEM((tt + 2 * CONV_HALO, w), F32)],
        compiler_params=_cparams("parallel"),
        name="conformer_conv",
    )(p, p, p, p, p, p, p, conv_w, conv_b, norm_g, norm_b)


def _merge_kernel(oa_ref, ob_ref, oc_ref, ga_ref, gb_ref, gc_ref, wa_ref, wb_ref, wc_ref, o_ref):
    m = jax.nn.sigmoid(ga_ref[...]) * _dot(oa_ref[...], wa_ref[...])
    m = m + jax.nn.sigmoid(gb_ref[...]) * _dot(ob_ref[...], wb_ref[...])
    m = m + jax.nn.sigmoid(gc_ref[...]) * _dot(oc_ref[...], wc_ref[...])
    o_ref[...] = m.astype(BF16)


def _merge(og_a, og_b, og_c, p, wa, wb, wc):
    t_tok = p.shape[0]
    tm = 256
    d = D_MODEL
    br = pl.BlockSpec((tm, NA_WIDTH), lambda i: (i, 0))
    gate = lambda col: pl.BlockSpec((tm, d), lambda i: (i, col // d))
    wspec = pl.BlockSpec((NA_WIDTH, d), lambda i: (0, 0))
    return pl.pallas_call(
        _merge_kernel,
        grid=(t_tok // tm,),
        in_specs=[br, br, br, gate(COL_GA), gate(COL_GB), gate(COL_GC), wspec, wspec, wspec],
        out_specs=pl.BlockSpec((tm, d), lambda i: (i, 0)),
        out_shape=jax.ShapeDtypeStruct((t_tok, d), BF16),
        compiler_params=_cparams("parallel"),
        name="merge",
    )(og_a, og_b, og_c, p, p, p, wa, wb, wc)


def _out_kernel(m_ref, w_ref, x_ref, mod_ref, g_ref, b_ref, o_ref):
    gate = mod_ref[0, :, 2 * D_MODEL:3 * D_MODEL]
    y = ALPHA * x_ref[...] + gate * _dot(m_ref[...], w_ref[...])
    mu = jnp.mean(y, -1, keepdims=True)
    yc = y - mu
    var = jnp.mean(yc * yc, -1, keepdims=True)
    o_ref[...] = yc * lax.rsqrt(var + LN_EPS) * g_ref[...] + b_ref[...]


def _out_projection(merged, w_out, x2d, mod, ln_g, ln_b, seq_len):
    t_tok = x2d.shape[0]
    tm = 256
    d = D_MODEL
    per = seq_len // tm if mod.shape[0] > 1 else t_tok // tm
    row = pl.BlockSpec((tm, d), lambda i: (i, 0))
    vec = pl.BlockSpec((1, d), lambda i: (0, 0))
    return pl.pallas_call(
        _out_kernel,
        grid=(t_tok // tm,),
        in_specs=[row, pl.BlockSpec((d, d), lambda i: (0, 0)), row,
                  pl.BlockSpec((1, 1, 3 * d), lambda i: (i // per, 0, 0)), vec, vec],
        out_specs=row,
        out_shape=jax.ShapeDtypeStruct((t_tok, d), F32),
        compiler_params=_cparams("parallel"),
        name="out_projection",
    )(merged, w_out, x2d, mod, ln_g, ln_b)


def kernel(x_prompt, x_sample, cache_k, cache_v, state_gla, c, c_ctx, w_mod, b_mod, w_in, rpb, gla_w_gate,
           gla_b_gate, gla_norm_g, conv_w, conv_b, conv_norm_g, conv_norm_b, w_proj_a, w_proj_b, w_proj_c,
           w_out, ln_g, ln_b):
    bsz, seq_len, d = x_prompt.shape
    dbsz, dseq, _ = x_sample.shape
    past = cache_k.shape[2]

    cvec = jnp.zeros((8, d), F32).at[0].set(c_ctx).at[1:1 + dbsz].set(c)
    mod_all = _modulation(cvec, w_mod, b_mod)

    w_main = jnp.concatenate([w_in[:, :, :LR_START], w_in[:, :, LR_START + 2 * GLA_LOWRANK:]], -1).astype(BF16)
    w_lr = jnp.pad(w_in[:, :, LR_START:LR_START + 2 * GLA_LOWRANK],
                   ((0, 0), (0, 0), (0, LANES - 2 * GLA_LOWRANK))).astype(BF16)
    wg = gla_w_gate.astype(BF16)
    wg_f = jnp.pad(wg[:, 0], ((0, 0), (0, LANES - GLA_LOWRANK), (0, 0)))
    wg_b = jnp.pad(wg[:, 1], ((0, 0), (GLA_LOWRANK, LANES - 2 * GLA_LOWRANK), (0, 0)))
    wpa, wpb, wpc, wo = (w.astype(BF16) for w in (w_proj_a, w_proj_b, w_proj_c, w_out))
    ck = cache_k.reshape(dbsz, DEPTH, past, NA_WIDTH)
    cv = cache_v.reshape(dbsz, DEPTH, past, NA_WIDTH)
    rope = _rope_tables(dseq)

    h_ctx = x_prompt.reshape(bsz * seq_len, d)
    h_lat = x_sample.reshape(dbsz * dseq, d)
    ks_list, vs_list, ss_list = [], [], []
    for l in range(DEPTH):
        mod_ctx = mod_all[l, 0:1][:, None, :]
        mod_lat = mod_all[l, 1:1 + dbsz][:, None, :]
        row = lambda a: a[l][None, :]
        gla_w = (wg_f[l], wg_b[l], gla_b_gate[l], row(gla_norm_g))
        conv_p = (conv_w[l], row(conv_b), row(conv_norm_g), row(conv_norm_b))

        def tail(x2d, p, og_a, og_b, mod, seq):
            og_c = _conformer_conv(p, *conv_p, seq)
            merged = _merge(og_a, og_b, og_c, p, wpa[l], wpb[l], wpc[l])
            return _out_projection(merged, wo[l], x2d, mod, row(ln_g), row(ln_b), seq)

        p, p_lr = _in_projection(h_ctx, mod_ctx, w_main[l], w_lr[l], seq_len)
        ks_list.append(p[:, COL_AK:COL_AK + NA_WIDTH].reshape(bsz, seq_len, NA_HEADS, NA_HEAD_DIM))
        vs_list.append(p[:, COL_AV:COL_AV + NA_WIDTH].reshape(bsz, seq_len, NA_HEADS, NA_HEAD_DIM))
        og_a = _context_attention(p, bsz, seq_len)
        og_b, s_fin = _gla(p, p_lr, *gla_w, bsz, seq_len)
        ss_list.append(s_fin)
        h_ctx = tail(h_ctx, p, og_a, og_b, mod_ctx, seq_len)

        p, p_lr = _in_projection(h_lat, mod_lat, w_main[l], w_lr[l], dseq)
        bias_tab = _na_bias_table(rpb[l], dseq // GRID_W)
        og_a = _neighbourhood_attention(p, ck, cv, bias_tab, l, dbsz, dseq)
        og_b = _gla(p, p_lr, *gla_w, dbsz, dseq, rope=rope, state=state_gla, layer=l)
        h_lat = tail(h_lat, p, og_a, og_b, mod_lat, dseq)

    return (h_ctx.reshape(bsz, seq_len, d), h_lat.reshape(dbsz, dseq, d),
            jnp.stack(ks_list, axis=1), jnp.stack(vs_list, axis=1), jnp.stack(ss_list, axis=1))
```

```python
import functools

import numpy as np
import jax
import jax.numpy as jnp
from jax import lax
from jax.experimental import pallas as pl
from jax.experimental.pallas import tpu as pltpu

D_MODEL = 2048
DEPTH = 2
GRID_W = 64
NA_HEADS = 8
NA_HEAD_DIM = 128
NA_WIDTH = NA_HEADS * NA_HEAD_DIM
NA_WIN_H = 8
NA_WIN_W = 16
GLA_HEADS = 4
GLA_DK = 128
GLA_DV = 256
GLA_KW = GLA_HEADS * GLA_DK
GLA_VW = GLA_HEADS * GLA_DV
GLA_LOWRANK = 16
GLA_TAU = 16.0
GLA_CHUNK = 64
ROPE_BASE = 10000.0
CONV_WIDTH = 1024
CONV_TAPS = 31
ALPHA = (2 * DEPTH) ** 0.25
LN_EPS = 1e-5
NEG_INF = -1e30

LANES = 128
SUBLANES = 8
CONV_HALO = 16
VMEM_LIMIT = 56 * 1024 * 1024

COL_AQ, COL_AK, COL_AV, COL_AZ = 0, 1024, 2048, 3072
COL_BQ, COL_BK, COL_BV, COL_BZ = 4096, 4608, 5120, 6144
COL_CU, COL_CG, COL_CZ = 7168, 8192, 9216
COL_GA, COL_GB, COL_GC = 10240, 12288, 14336
N_MAIN = 16384
LR_START = 6144

BF16 = jnp.bfloat16
F32 = jnp.float32


def _cparams(*sem):
    return pltpu.CompilerParams(dimension_semantics=sem, vmem_limit_bytes=VMEM_LIMIT)


def _silu(x):
    return x * jax.nn.sigmoid(x)


def _dot(a, b):
    return jnp.dot(a, b, preferred_element_type=F32)


def _dot_nt(a, b):
    return lax.dot_general(a, b, (((1,), (1,)), ((), ())), preferred_element_type=F32)


def _mod_kernel(cv_ref, w_ref, b_ref, o_ref):
    s = _silu(cv_ref[...]).astype(BF16)
    o_ref[0] = _dot(s, w_ref[0].astype(BF16)) + b_ref[0]


def _modulation(cvec, w_mod, b_mod):
    tn = 1024
    n3 = 3 * D_MODEL
    return pl.pallas_call(
        _mod_kernel,
        grid=(DEPTH, n3 // tn),
        in_specs=[pl.BlockSpec((8, D_MODEL), lambda l, n: (0, 0)),
                  pl.BlockSpec((1, D_MODEL, tn), lambda l, n: (l, 0, n)),
                  pl.BlockSpec((1, 1, tn), lambda l, n: (l, 0, n))],
        out_specs=pl.BlockSpec((1, 8, tn), lambda l, n: (l, 0, n)),
        out_shape=jax.ShapeDtypeStruct((DEPTH, 8, n3), F32),
        compiler_params=_cparams("parallel", "parallel"),
        name="modulation",
    )(cvec, w_mod, b_mod.reshape(DEPTH, 1, n3))


def _inproj_kernel(x_ref, mod_ref, w_ref, wlr_ref, o_ref, olr_ref, h_scr):
    strip = 256

    @pl.when(pl.program_id(1) == 0)
    def _():
        shift = mod_ref[0, :, 0:D_MODEL]
        scale = mod_ref[0, :, D_MODEL:2 * D_MODEL]

        def body(i, carry):
            sl = pl.ds(pl.multiple_of(i * strip, strip), strip)
            x = x_ref[sl, :]
            mu = jnp.mean(x, -1, keepdims=True)
            xc = x - mu
            var = jnp.mean(xc * xc, -1, keepdims=True)
            y = xc * lax.rsqrt(var + LN_EPS)
            hb = (y * (1.0 + scale) + shift).astype(BF16)
            h_scr[sl, :] = hb
            olr_ref[sl, :] = _dot(hb, wlr_ref[...])
            return carry

        lax.fori_loop(0, x_ref.shape[0] // strip, body, 0)

    o_ref[...] = _dot(h_scr[...], w_ref[...])


def _in_projection(x2d, mod, w_main, w_lr, seq_len):
    t_tok = x2d.shape[0]
    tm, tn = 1024, 1024
    per = seq_len // tm if mod.shape[0] > 1 else t_tok // tm
    return pl.pallas_call(
        _inproj_kernel,
        grid=(t_tok // tm, N_MAIN // tn),
        in_specs=[pl.BlockSpec((tm, D_MODEL), lambda m, n: (m, 0)),
                  pl.BlockSpec((1, 1, 3 * D_MODEL), lambda m, n: (m // per, 0, 0)),
                  pl.BlockSpec((D_MODEL, tn), lambda m, n: (0, n)),
                  pl.BlockSpec((D_MODEL, LANES), lambda m, n: (0, 0))],
        out_specs=[pl.BlockSpec((tm, tn), lambda m, n: (m, n)),
                   pl.BlockSpec((tm, LANES), lambda m, n: (m, 0))],
        out_shape=[jax.ShapeDtypeStruct((t_tok, N_MAIN), F32),
                   jax.ShapeDtypeStruct((t_tok, LANES), F32)],
        scratch_shapes=[pltpu.VMEM((tm, D_MODEL), BF16)],
        compiler_params=_cparams("parallel", "arbitrary"),
        name="in_projection",
    )(x2d, mod, w_main, w_lr)


def _ctx_attn_kernel(q_ref, k_ref, v_ref, z_ref, *rest):
    o_ref, ko_ref, vo_ref = rest[-3:]
    scale = NA_HEAD_DIM ** -0.5
    ko_ref[...] = k_ref[...]
    vo_ref[...] = v_ref[...]
    for h in range(NA_HEADS):
        cs = slice(h * NA_HEAD_DIM, (h + 1) * NA_HEAD_DIM)
        q = q_ref[:, cs].astype(BF16)
        k = k_ref[:, cs].astype(BF16)
        v = v_ref[:, cs].astype(BF16)
        s = _dot_nt(q, k) * scale
        e = jnp.exp(s - jnp.max(s, -1, keepdims=True))
        p = e / jnp.sum(e, -1, keepdims=True)
        o = _dot(p.astype(BF16), v)
        o_ref[:, cs] = (o * _silu(z_ref[:, cs])).astype(BF16)


def _context_attention(p_ctx, bsz, seq_len, layer, kv_prev):
    blk = lambda j: pl.BlockSpec((seq_len, NA_WIDTH), lambda b: (b, j))
    kv_spec = pl.BlockSpec((None, None, seq_len, NA_WIDTH), lambda b: (b, layer, 0, 0))
    kv_shape = jax.ShapeDtypeStruct((bsz, DEPTH, seq_len, NA_WIDTH), F32)
    return pl.pallas_call(
        _ctx_attn_kernel,
        grid=(bsz,),
        in_specs=[blk(COL_AQ // NA_WIDTH), blk(COL_AK // NA_WIDTH), blk(COL_AV // NA_WIDTH), blk(COL_AZ // NA_WIDTH)]
        + [pl.BlockSpec(memory_space=pl.ANY)] * len(kv_prev),
        out_specs=[pl.BlockSpec((seq_len, NA_WIDTH), lambda b: (b, 0)), kv_spec, kv_spec],
        out_shape=[jax.ShapeDtypeStruct((bsz * seq_len, NA_WIDTH), BF16), kv_shape, kv_shape],
        input_output_aliases={4 + i: 1 + i for i in range(len(kv_prev))},
        compiler_params=_cparams("parallel"),
        name="context_attention",
    )(p_ctx, p_ctx, p_ctx, p_ctx, *kv_prev)


NA_QROWS = 4
NA_KROWS = NA_QROWS + NA_WIN_H


def _na_plan(rows):
    kh = min(NA_WIN_H, rows)
    patterns, groups = [], []
    for r0 in range(0, rows, NA_QROWS):
        ks = int(np.clip(r0 - kh // 2, 0, rows - NA_KROWS))
        pat = []
        for rq in range(NA_QROWS):
            r = r0 + rq
            rs = int(np.clip(r - kh // 2, 0, rows - kh))
            pat.append(tuple((ks + i - r) if rs <= ks + i < rs + kh else None for i in range(NA_KROWS)))
        pat = tuple(pat)
        if pat not in patterns:
            patterns.append(pat)
        groups.append((r0, ks, patterns.index(pat)))
    return tuple(groups), patterns


def _na_bias_table(rpb_l, patterns):
    nh = rpb_l.shape[0]
    n_dr = 2 * NA_WIN_H - 1
    c = np.arange(GRID_W)
    cs = np.clip(c - NA_WIN_W // 2, 0, GRID_W - NA_WIN_W)
    valid = (c[None, :] >= cs[:, None]) & (c[None, :] < cs[:, None] + NA_WIN_W)
    span = 2 * GRID_W
    left = (GRID_W - 1) - (NA_WIN_W - 1)
    wv = jnp.pad(rpb_l.astype(F32), ((0, 0), (0, 0), (left, span - left - (2 * NA_WIN_W - 1))))
    skew = jnp.tile(wv, (1, 1, GRID_W))[..., :GRID_W * (span - 1)].reshape(nh, n_dr, GRID_W, span - 1)
    toe = jnp.where(valid, skew[..., GRID_W - 1:], NEG_INF)
    neg = jnp.full((nh, GRID_W, GRID_W), NEG_INF, F32)
    tabs = []
    for pat in patterns:
        per_q = [jnp.stack([neg if dr is None else toe[:, dr + NA_WIN_H - 1] for dr in prow], axis=2)
                 for prow in pat]
        tabs.append(jnp.stack(per_q, axis=1).reshape(nh, NA_QROWS * GRID_W, NA_KROWS * GRID_W))
    return jnp.stack(tabs, axis=1)


def _na_kernel(q_ref, k_ref, v_ref, z_ref, kc_ref, vc_ref, bias_ref, o_ref, kb_scr, vb_scr, *, groups):
    scale = NA_HEAD_DIM ** -0.5
    kb_scr[...] = k_ref[...].astype(BF16)
    vb_scr[...] = v_ref[...].astype(BF16)
    kctx = kc_ref[...].astype(BF16)
    vctx = vc_ref[...].astype(BF16)
    nq, nk = NA_QROWS * GRID_W, NA_KROWS * GRID_W
    for r0, ks, pat in groups:
        qsl = slice(r0 * GRID_W, r0 * GRID_W + nq)
        ksl = slice(ks * GRID_W, ks * GRID_W + nk)
        q = q_ref[qsl, :].astype(BF16)
        s_win = _dot_nt(q, kb_scr[ksl, :]) * scale + bias_ref[pat]
        s_ctx = _dot_nt(q, kctx) * scale
        m = jnp.maximum(jnp.max(s_win, -1, keepdims=True), jnp.max(s_ctx, -1, keepdims=True))
        e_win = jnp.exp(s_win - m)
        e_ctx = jnp.exp(s_ctx - m)
        den = jnp.sum(e_win, -1, keepdims=True) + jnp.sum(e_ctx, -1, keepdims=True)
        o = _dot((e_win / den).astype(BF16), vb_scr[ksl, :]) + _dot((e_ctx / den).astype(BF16), vctx)
        o_ref[qsl, :] = (o * _silu(z_ref[qsl, :])).astype(BF16)


def _neighbourhood_attention(p_lat, cache_k, cache_v, rpb_l, layer, bsz, n_tok):
    rows = n_tok // GRID_W
    assert rows % NA_QROWS == 0 and rows >= NA_KROWS
    groups, patterns = _na_plan(rows)
    bias_tab = _na_bias_table(rpb_l, patterns)
    past = cache_k.shape[2]
    hd = NA_HEAD_DIM
    blk = lambda col: pl.BlockSpec((n_tok, hd), lambda b, h: (b, col // hd + h))
    cblk = pl.BlockSpec((None, None, past, hd), lambda b, h: (b, layer, 0, h))
    return pl.pallas_call(
        functools.partial(_na_kernel, groups=groups),
        grid=(bsz, NA_HEADS),
        in_specs=[blk(COL_AQ), blk(COL_AK), blk(COL_AV), blk(COL_AZ), cblk, cblk,
                  pl.BlockSpec((None,) + bias_tab.shape[1:], lambda b, h: (h, 0, 0, 0))],
        out_specs=pl.BlockSpec((n_tok, hd), lambda b, h: (b, h)),
        out_shape=jax.ShapeDtypeStruct((bsz * n_tok, NA_WIDTH), BF16),
        scratch_shapes=[pltpu.VMEM((n_tok, hd), BF16), pltpu.VMEM((n_tok, hd), BF16)],
        compiler_params=_cparams("parallel", "parallel"),
        name="neighbourhood_attention",
    )(p_lat, p_lat, p_lat, p_lat, cache_k, cache_v, bias_tab)


def _rope_tables(seq_len):
    t = jnp.arange(seq_len)
    half = GLA_DK // 2
    inv = ROPE_BASE ** (-jnp.arange(0, half, 2, dtype=F32) / half)

    def tab(pos):
        ang = pos.astype(F32)[:, None] * inv[None, :]
        return jnp.cos(ang), jnp.sin(ang)

    cr, sr = tab(t // GRID_W)
    cc, sc = tab(t % GRID_W)
    cos = jnp.concatenate([cr, cr, cc, cc], -1)
    sin = jnp.concatenate([-sr, sr, -sc, sc], -1)
    return cos, sin


def _rope(x, cos, sin):
    quarter = GLA_DK // 4
    lane = lax.broadcasted_iota(jnp.int32, x.shape, 1)
    partner = jnp.where((lane % (2 * quarter)) < quarter,
                        pltpu.roll(x, GLA_DK - quarter, 1), pltpu.roll(x, quarter, 1))
    return x * cos + partner * sin


def _log_sigmoid(x):
    return jnp.minimum(x, 0.0) - jnp.log1p(jnp.exp(-jnp.abs(x)))


def _chunk_scan(x, reverse):
    n = x.shape[0]
    pos = lax.broadcasted_iota(jnp.int32, x.shape, 0) % GLA_CHUNK
    step = 1
    while step < GLA_CHUNK:
        if reverse:
            x = x + jnp.where(pos < GLA_CHUNK - step, pltpu.roll(x, n - step, 0), 0.0)
        else:
            x = x + jnp.where(pos >= step, pltpu.roll(x, step, 0), 0.0)
        step *= 2
    return x


GLA_GROUP = 4


def _gla_kernel(*refs, latent, n_groups):
    (q_ref, k_ref, v_ref, z_ref, lr_ref, wgf_ref, wgb_ref, bg_ref, ng_ref), rest = refs[:9], refs[9:]
    if latent:
        cos_ref, sin_ref, s0_ref, o_ref, qs, ks, laf, lab, o_f, o_b, st_f, st_b = rest
    else:
        sprev_ref = rest[:-10]
        o_ref, sfin_ref, qs, ks, laf, lab, o_f, o_b, st_f, st_b = rest[-10:]
    c_len = GLA_CHUNK
    g_len = GLA_GROUP * c_len

    q = q_ref[...] * (GLA_DK ** -0.5)
    k = k_ref[...]
    if latent:
        q = _rope(q, cos_ref[...], sin_ref[...])
        k = _rope(k, cos_ref[...], sin_ref[...])
    qs[...] = q
    ks[...] = k
    lrb = lr_ref[...].astype(BF16)
    laf[...] = _log_sigmoid(_dot(lrb, wgf_ref[...]) + bg_ref[0:1, :]) * (1.0 / GLA_TAU)
    lab[...] = _log_sigmoid(_dot(lrb, wgb_ref[...]) + bg_ref[1:2, :]) * (1.0 / GLA_TAU)

    ii = lax.broadcasted_iota(jnp.int32, (c_len, c_len), 0)
    jj = lax.broadcasted_iota(jnp.int32, (c_len, c_len), 1)

    def half_group(g, reverse):
        la_ref, out, st = (lab, o_b, st_b) if reverse else (laf, o_f, st_f)
        mask = (ii <= jj) if reverse else (ii >= jj)
        edge = 0 if reverse else c_len - 1
        start = g * g_len if isinstance(g, int) else pl.multiple_of(g * g_len, g_len)
        b_all = _chunk_scan(la_ref[pl.ds(start, g_len), :], reverse)
        s_t = st[...]
        for c in (range(GLA_GROUP - 1, -1, -1) if reverse else range(GLA_GROUP)):
            sl = pl.ds(start + c * c_len, c_len)
            b = b_all[c * c_len:(c + 1) * c_len]
            btot = b[edge:edge + 1, :]
            qc = qs[sl, :]
            kc = ks[sl, :]
            vc = v_ref[sl, :]
            qe = (qc * jnp.exp(b)).astype(BF16)
            ke = (kc * jnp.exp(-b)).astype(BF16)
            kw = (kc * jnp.exp(btot - b)).astype(BF16)
            att = jnp.where(mask, _dot_nt(qe, ke), 0.0)
            out[sl, :] = _dot(att.astype(BF16), vc.astype(BF16)) + _dot_nt(qe, s_t.astype(BF16))
            s_t = jnp.exp(btot) * s_t + _dot(vc.T.astype(BF16), kw)
        st[...] = s_t

    for direction, st in enumerate((st_f, st_b)):
        st[...] = s0_ref[direction].T if latent else jnp.zeros_like(st)
    if n_groups == 1:
        half_group(0, False)
        half_group(0, True)
    else:
        def body(i, carry):
            half_group(i, False)
            half_group(n_groups - 1 - i, True)
            return carry
        lax.fori_loop(0, n_groups, body, 0)
    if not latent:
        sfin_ref[0] = st_f[...].T
        sfin_ref[1] = st_b[...].T

    v = v_ref[...]
    o = (o_f[...] + o_b[...]) - jnp.sum(qs[...] * ks[...], -1, keepdims=True) * v
    o = o * lax.rsqrt(jnp.mean(o * o, -1, keepdims=True) + LN_EPS) * ng_ref[...]
    o_ref[...] = (o * _silu(z_ref[...])).astype(BF16)


def _gla(p, p_lr, wg_f, wg_b, b_gate, norm_g, bsz, seq_len, layer, rope=None, state=None, state_prev=()):
    latent = state is not None
    assert seq_len % (GLA_GROUP * GLA_CHUNK) == 0
    dk, dv = GLA_DK, GLA_DV
    in_specs = [pl.BlockSpec((seq_len, dk), lambda b, h: (b, COL_BQ // dk + h)),
                pl.BlockSpec((seq_len, dk), lambda b, h: (b, COL_BK // dk + h)),
                pl.BlockSpec((seq_len, dv), lambda b, h: (b, COL_BV // dv + h)),
                pl.BlockSpec((seq_len, dv), lambda b, h: (b, COL_BZ // dv + h)),
                pl.BlockSpec((seq_len, LANES), lambda b, h: (b, 0)),
                pl.BlockSpec((LANES, dk), lambda b, h: (0, h)),
                pl.BlockSpec((LANES, dk), lambda b, h: (0, h)),
                pl.BlockSpec((2, dk), lambda b, h: (0, h)),
                pl.BlockSpec((1, dv), lambda b, h: (0, h))]
    args = [p, p, p, p, p_lr, wg_f, wg_b, b_gate, norm_g]
    o_spec = pl.BlockSpec((seq_len, dv), lambda b, h: (b, h))
    o_shape = jax.ShapeDtypeStruct((bsz * seq_len, GLA_VW), BF16)
    if latent:
        cos, sin = rope
        in_specs += [pl.BlockSpec((seq_len, dk), lambda b, h: (0, 0)),
                     pl.BlockSpec((seq_len, dk), lambda b, h: (0, 0)),
                     pl.BlockSpec((None, None, 2, None, dk, dv), lambda b, h: (b, layer, 0, h, 0, 0))]
        args += [cos, sin, state]
        out_specs, out_shape, aliases = o_spec, o_shape, {}
    else:
        aliases = {len(args) + i: 1 + i for i in range(len(state_prev))}
        in_specs += [pl.BlockSpec(memory_space=pl.ANY)] * len(state_prev)
        args += list(state_prev)
        out_specs = [o_spec, pl.BlockSpec((None, None, 2, None, dk, dv), lambda b, h: (b, layer, 0, h, 0, 0))]
        out_shape = [o_shape, jax.ShapeDtypeStruct((bsz, DEPTH, 2, GLA_HEADS, dk, dv), F32)]
    return pl.pallas_call(
        functools.partial(_gla_kernel, latent=latent, n_groups=seq_len // (GLA_GROUP * GLA_CHUNK)),
        grid=(bsz, GLA_HEADS),
        in_specs=in_specs,
        out_specs=out_specs,
        out_shape=out_shape,
        input_output_aliases=aliases,
        scratch_shapes=[pltpu.VMEM((seq_len, dk), F32), pltpu.VMEM((seq_len, dk), F32),
                        pltpu.VMEM((seq_len, dk), F32), pltpu.VMEM((seq_len, dk), F32),
                        pltpu.VMEM((seq_len, dv), F32), pltpu.VMEM((seq_len, dv), F32),
                        pltpu.VMEM((dv, dk), F32), pltpu.VMEM((dv, dk), F32)],
        compiler_params=_cparams("parallel", "parallel"),
        name="gla_latent" if latent else "gla_context",
    )(*args)


def _conv_kernel(u_ref, g_ref, up_ref, gp_ref, un_ref, gn_ref, z_ref, w_ref, b_ref, ng_ref, nb_ref, o_ref, pad, cv,
                 *, tiles_per_seq, tt):
    t = pl.program_id(0) % tiles_per_seq
    halo = CONV_HALO
    glu = lambda u, g: u * jax.nn.sigmoid(g)
    pad[halo:halo + tt, :] = glu(u_ref[...], g_ref[...])
    pad[0:halo, :] = jnp.where(t > 0, glu(up_ref[...], gp_ref[...]), 0.0)
    pad[halo + tt:halo + tt + halo, :] = jnp.where(t < tiles_per_seq - 1, glu(un_ref[...], gn_ref[...]), 0.0)
    first = halo - CONV_TAPS // 2
    rb = 64
    reach = -(-(first + CONV_TAPS - 1) // SUBLANES) * SUBLANES
    for c0 in range(0, CONV_WIDTH, LANES):
        lanes = slice(c0, c0 + LANES)
        w_strip = w_ref[:, lanes]
        for r0 in range(0, tt, rb):
            base = pad[r0:r0 + rb + reach, lanes]
            acc = jnp.broadcast_to(b_ref[:, lanes], (rb, LANES))
            for phase in range(SUBLANES):
                rows = pltpu.roll(base, rb + reach - phase, 0) if phase else base
                for a in range(0, reach, SUBLANES):
                    j = a + phase - first
                    if 0 <= j < CONV_TAPS:
                        acc = acc + rows[a:a + rb] * w_strip[j:j + 1, :]
            cv[r0:r0 + rb, lanes] = acc
    acc = cv[...]
    mu = jnp.mean(acc, -1, keepdims=True)
    xc = acc - mu
    var = jnp.mean(xc * xc, -1, keepdims=True)
    y = xc * lax.rsqrt(var + LN_EPS) * ng_ref[...] + nb_ref[...]
    o_ref[...] = (_silu(y) * _silu(z_ref[...])).astype(BF16)


def _conformer_conv(p, conv_w, conv_b, norm_g, norm_b, seq_len):
    t_tok = p.shape[0]
    tt = 256
    tiles_per_seq = seq_len // tt
    w = CONV_WIDTH
    hb = tt // CONV_HALO
    n_halo = t_tok // CONV_HALO
    main = lambda col: pl.BlockSpec((tt, w), lambda i: (i, col // w))
    prev = lambda col: pl.BlockSpec((CONV_HALO, w), lambda i: (jnp.maximum(i * hb - 1, 0), col // w))
    nxt = lambda col: pl.BlockSpec((CONV_HALO, w), lambda i: (jnp.minimum((i + 1) * hb, n_halo - 1), col // w))
    vec = pl.BlockSpec((1, w), lambda i: (0, 0))
    return pl.pallas_call(
        functools.partial(_conv_kernel, tiles_per_seq=tiles_per_seq, tt=tt),
        grid=(t_tok // tt,),
        in_specs=[main(COL_CU), main(COL_CG), prev(COL_CU), prev(COL_CG), nxt(COL_CU), nxt(COL_CG), main(COL_CZ),
                  pl.BlockSpec((CONV_TAPS, w), lambda i: (0, 0)), vec, vec, vec],
        out_specs=pl.BlockSpec((tt, w), lambda i: (i, 0)),
        out_shape=jax.ShapeDtypeStruct((t_tok, w), BF16),
        scratch_shapes=[pltpu.VMEM((tt + 2 * CONV_HALO, w), F32), pltpu.VMEM((tt, w), F32)],
        compiler_params=_cparams("parallel"),
        name="conformer_conv",
    )(p, p, p, p, p, p, p, conv_w, conv_b, norm_g, norm_b)


def _merge_kernel(oa_ref, ob_ref, oc_ref, ga_ref, gb_ref, gc_ref, wa_ref, wb_ref, wc_ref, o_ref):
    m = jax.nn.sigmoid(ga_ref[...]) * _dot(oa_ref[...], wa_ref[...])
    m = m + jax.nn.sigmoid(gb_ref[...]) * _dot(ob_ref[...], wb_ref[...])
    m = m + jax.nn.sigmoid(gc_ref[...]) * _dot(oc_ref[...], wc_ref[...])
    o_ref[...] = m.astype(BF16)


def _merge(og_a, og_b, og_c, p, wa, wb, wc):
    t_tok = p.shape[0]
    tm = 256
    d = D_MODEL
    br = pl.BlockSpec((tm, NA_WIDTH), lambda i: (i, 0))
    gate = lambda col: pl.BlockSpec((tm, d), lambda i: (i, col // d))
    wspec = pl.BlockSpec((NA_WIDTH, d), lambda i: (0, 0))
    return pl.pallas_call(
        _merge_kernel,
        grid=(t_tok // tm,),
        in_specs=[br, br, br, gate(COL_GA), gate(COL_GB), gate(COL_GC), wspec, wspec, wspec],
        out_specs=pl.BlockSpec((tm, d), lambda i: (i, 0)),
        out_shape=jax.ShapeDtypeStruct((t_tok, d), BF16),
        compiler_params=_cparams("parallel"),
        name="merge",
    )(og_a, og_b, og_c, p, p, p, wa, wb, wc)


def _out_kernel(m_ref, w_ref, x_ref, mod_ref, g_ref, b_ref, o_ref):
    gate = mod_ref[0, :, 2 * D_MODEL:3 * D_MODEL]
    y = ALPHA * x_ref[...] + gate * _dot(m_ref[...], w_ref[...])
    mu = jnp.mean(y, -1, keepdims=True)
    yc = y - mu
    var = jnp.mean(yc * yc, -1, keepdims=True)
    o_ref[...] = yc * lax.rsqrt(var + LN_EPS) * g_ref[...] + b_ref[...]


def _out_projection(merged, w_out, x2d, mod, ln_g, ln_b, seq_len):
    t_tok = x2d.shape[0]
    tm = 256
    d = D_MODEL
    per = seq_len // tm if mod.shape[0] > 1 else t_tok // tm
    row = pl.BlockSpec((tm, d), lambda i: (i, 0))
    vec = pl.BlockSpec((1, d), lambda i: (0, 0))
    return pl.pallas_call(
        _out_kernel,
        grid=(t_tok // tm,),
        in_specs=[row, pl.BlockSpec((d, d), lambda i: (0, 0)), row,
                  pl.BlockSpec((1, 1, 3 * d), lambda i: (i // per, 0, 0)), vec, vec],
        out_specs=row,
        out_shape=jax.ShapeDtypeStruct((t_tok, d), F32),
        compiler_params=_cparams("parallel"),
        name="out_projection",
    )(merged, w_out, x2d, mod, ln_g, ln_b)


def kernel(x_prompt, x_sample, cache_k, cache_v, state_gla, c, c_ctx, w_mod, b_mod, w_in, rpb, gla_w_gate,
           gla_b_gate, gla_norm_g, conv_w, conv_b, conv_norm_g, conv_norm_b, w_proj_a, w_proj_b, w_proj_c,
           w_out, ln_g, ln_b):
    bsz, seq_len, d = x_prompt.shape
    dbsz, dseq, _ = x_sample.shape
    past = cache_k.shape[2]

    cvec = jnp.zeros((8, d), F32).at[0].set(c_ctx).at[1:1 + dbsz].set(c)
    mod_all = _modulation(cvec, w_mod, b_mod)

    w_main = jnp.concatenate([w_in[:, :, :LR_START], w_in[:, :, LR_START + 2 * GLA_LOWRANK:]], -1).astype(BF16)
    w_lr = jnp.pad(w_in[:, :, LR_START:LR_START + 2 * GLA_LOWRANK],
                   ((0, 0), (0, 0), (0, LANES - 2 * GLA_LOWRANK))).astype(BF16)
    wg = gla_w_gate.astype(BF16)
    wg_f = jnp.pad(wg[:, 0], ((0, 0), (0, LANES - GLA_LOWRANK), (0, 0)))
    wg_b = jnp.pad(wg[:, 1], ((0, 0), (GLA_LOWRANK, LANES - 2 * GLA_LOWRANK), (0, 0)))
    wpa, wpb, wpc, wo = (w.astype(BF16) for w in (w_proj_a, w_proj_b, w_proj_c, w_out))
    ck = cache_k.reshape(dbsz, DEPTH, past, NA_WIDTH)
    cv = cache_v.reshape(dbsz, DEPTH, past, NA_WIDTH)
    rope = _rope_tables(dseq)

    h_ctx = x_prompt.reshape(bsz * seq_len, d)
    h_lat = x_sample.reshape(dbsz * dseq, d)
    kv_new, s_new = (), ()
    for l in range(DEPTH):
        mod_ctx = mod_all[l, 0:1][:, None, :]
        mod_lat = mod_all[l, 1:1 + dbsz][:, None, :]
        row = lambda a: a[l][None, :]
        gla_w = (wg_f[l], wg_b[l], gla_b_gate[l], row(gla_norm_g))
        conv_p = (conv_w[l], row(conv_b), row(conv_norm_g), row(conv_norm_b))

        def tail(x2d, p, og_a, og_b, mod, seq):
            og_c = _conformer_conv(p, *conv_p, seq)
            merged = _merge(og_a, og_b, og_c, p, wpa[l], wpb[l], wpc[l])
            return _out_projection(merged, wo[l], x2d, mod, row(ln_g), row(ln_b), seq)

        p, p_lr = _in_projection(h_ctx, mod_ctx, w_main[l], w_lr[l], seq_len)
        og_a, *kv_new = _context_attention(p, bsz, seq_len, l, kv_new)
        og_b, *s_new = _gla(p, p_lr, *gla_w, bsz, seq_len, l, state_prev=s_new)
        h_ctx = tail(h_ctx, p, og_a, og_b, mod_ctx, seq_len)

        p, p_lr = _in_projection(h_lat, mod_lat, w_main[l], w_lr[l], dseq)
        og_a = _neighbourhood_attention(p, ck, cv, rpb[l], l, dbsz, dseq)
        og_b = _gla(p, p_lr, *gla_w, dbsz, dseq, l, rope=rope, state=state_gla)
        h_lat = tail(h_lat, p, og_a, og_b, mod_lat, dseq)

    new_k, new_v = (a.reshape(bsz, DEPTH, seq_len, NA_HEADS, NA_HEAD_DIM) for a in kv_new)
    return (h_ctx.reshape(bsz, seq_len, d), h_lat.reshape(dbsz, dseq, d), new_k, new_v, s_new[0])
```

```python
import functools

import numpy as np
import jax
import jax.numpy as jnp
from jax import lax
from jax.experimental import pallas as pl
from jax.experimental.pallas import tpu as pltpu

D_MODEL = 2048
DEPTH = 2
GRID_W = 64
NA_HEADS = 8
NA_HEAD_DIM = 128
NA_WIDTH = NA_HEADS * NA_HEAD_DIM
NA_WIN_H = 8
NA_WIN_W = 16
GLA_HEADS = 4
GLA_DK = 128
GLA_DV = 256
GLA_KW = GLA_HEADS * GLA_DK
GLA_VW = GLA_HEADS * GLA_DV
GLA_LOWRANK = 16
GLA_TAU = 16.0
GLA_CHUNK = 64
ROPE_BASE = 10000.0
CONV_WIDTH = 1024
CONV_TAPS = 31
ALPHA = (2 * DEPTH) ** 0.25
LN_EPS = 1e-5
NEG_INF = -1e30

LANES = 128
SUBLANES = 8
CONV_HALO = 16
VMEM_LIMIT = 56 * 1024 * 1024

COL_AQ, COL_AK, COL_AV, COL_AZ = 0, 1024, 2048, 3072
COL_BQ, COL_BK, COL_BV, COL_BZ = 4096, 4608, 5120, 6144
COL_CU, COL_CG, COL_CZ = 7168, 8192, 9216
COL_GA, COL_GB, COL_GC = 10240, 12288, 14336
N_MAIN = 16384
LR_START = 6144

BF16 = jnp.bfloat16
F32 = jnp.float32


def _cparams(*sem):
    return pltpu.CompilerParams(dimension_semantics=sem, vmem_limit_bytes=VMEM_LIMIT)


def _silu(x):
    return x * jax.nn.sigmoid(x)


def _dot(a, b):
    return jnp.dot(a, b, preferred_element_type=F32)


def _dot_nt(a, b):
    return lax.dot_general(a, b, (((1,), (1,)), ((), ())), preferred_element_type=F32)


def _mod_kernel(cv_ref, w_ref, b_ref, o_ref):
    s = _silu(cv_ref[...]).astype(BF16)
    o_ref[0] = _dot(s, w_ref[0].astype(BF16)) + b_ref[0]


def _modulation(cvec, w_mod, b_mod):
    tn = 1024
    n3 = 3 * D_MODEL
    return pl.pallas_call(
        _mod_kernel,
        grid=(DEPTH, n3 // tn),
        in_specs=[pl.BlockSpec((8, D_MODEL), lambda l, n: (0, 0)),
                  pl.BlockSpec((1, D_MODEL, tn), lambda l, n: (l, 0, n)),
                  pl.BlockSpec((1, 1, tn), lambda l, n: (l, 0, n))],
        out_specs=pl.BlockSpec((1, 8, tn), lambda l, n: (l, 0, n)),
        out_shape=jax.ShapeDtypeStruct((DEPTH, 8, n3), F32),
        compiler_params=_cparams("parallel", "parallel"),
        name="modulation",
    )(cvec, w_mod, b_mod.reshape(DEPTH, 1, n3))


def _regroup_kernel(a_ref, b_ref, o_ref, olr_ref, *, lr_tile, lr_width):
    n = pl.program_id(1)
    tn = a_ref.shape[2]

    @pl.when(n < lr_tile)
    def _():
        o_ref[0] = a_ref[0].astype(BF16)

    @pl.when(n >= lr_tile)
    def _():
        lane = lax.broadcasted_iota(jnp.int32, (a_ref.shape[1], LANES), 1)
        keep = lane < LANES - lr_width
        nxt = pltpu.roll(a_ref[0, :, 0:LANES], LANES - lr_width, 1)
        for c0 in range(0, tn, LANES):
            cur = nxt
            blk = a_ref[0, :, c0 + LANES:c0 + 2 * LANES] if c0 + LANES < tn else b_ref[0]
            nxt = pltpu.roll(blk, LANES - lr_width, 1)
            o_ref[0, :, c0:c0 + LANES] = jnp.where(keep, cur, nxt).astype(BF16)

    @pl.when(n == lr_tile)
    def _():
        lane = lax.broadcasted_iota(jnp.int32, (a_ref.shape[1], LANES), 1)
        olr_ref[0] = jnp.where(lane < lr_width, a_ref[0, :, 0:LANES], 0.0).astype(BF16)


def _regroup_w_in(w_in):
    tn = 1024
    assert LR_START % tn == 0 and N_MAIN % tn == 0
    last = (w_in.shape[2] - 1) // LANES
    return pl.pallas_call(
        functools.partial(_regroup_kernel, lr_tile=LR_START // tn, lr_width=2 * GLA_LOWRANK),
        grid=(DEPTH, N_MAIN // tn),
        in_specs=[pl.BlockSpec((1, D_MODEL, tn), lambda l, n: (l, 0, n)),
                  pl.BlockSpec((1, D_MODEL, LANES), lambda l, n: (l, 0, jnp.minimum((n + 1) * (tn // LANES), last)))],
        out_specs=[pl.BlockSpec((1, D_MODEL, tn), lambda l, n: (l, 0, n)),
                   pl.BlockSpec((1, D_MODEL, LANES), lambda l, n: (l, 0, 0))],
        out_shape=[jax.ShapeDtypeStruct((DEPTH, D_MODEL, N_MAIN), BF16),
                   jax.ShapeDtypeStruct((DEPTH, D_MODEL, LANES), BF16)],
        compiler_params=_cparams("parallel", "arbitrary"),
        name="regroup_w_in",
    )(w_in, w_in)


def _inproj_kernel(x_ref, mod_ref, w_ref, wlr_ref, o_ref, olr_ref, h_scr):
    strip = 256

    @pl.when(pl.program_id(1) == 0)
    def _():
        shift = mod_ref[0, :, 0:D_MODEL]
        scale = mod_ref[0, :, D_MODEL:2 * D_MODEL]

        def body(i, carry):
            sl = pl.ds(pl.multiple_of(i * strip, strip), strip)
            x = x_ref[sl, :]
            mu = jnp.mean(x, -1, keepdims=True)
            xc = x - mu
            var = jnp.mean(xc * xc, -1, keepdims=True)
            y = xc * lax.rsqrt(var + LN_EPS)
            hb = (y * (1.0 + scale) + shift).astype(BF16)
            h_scr[sl, :] = hb
            olr_ref[sl, :] = _dot(hb, wlr_ref[...])
            return carry

        lax.fori_loop(0, x_ref.shape[0] // strip, body, 0)

    o_ref[...] = _dot(h_scr[...], w_ref[...])


def _in_projection(x2d, mod, w_main, w_lr, seq_len):
    t_tok = x2d.shape[0]
    tm, tn = 1024, 1024
    per = seq_len // tm if mod.shape[0] > 1 else t_tok // tm
    return pl.pallas_call(
        _inproj_kernel,
        grid=(t_tok // tm, N_MAIN // tn),
        in_specs=[pl.BlockSpec((tm, D_MODEL), lambda m, n: (m, 0)),
                  pl.BlockSpec((1, 1, 3 * D_MODEL), lambda m, n: (m // per, 0, 0)),
                  pl.BlockSpec((D_MODEL, tn), lambda m, n: (0, n)),
                  pl.BlockSpec((D_MODEL, LANES), lambda m, n: (0, 0))],
        out_specs=[pl.BlockSpec((tm, tn), lambda m, n: (m, n)),
                   pl.BlockSpec((tm, LANES), lambda m, n: (m, 0))],
        out_shape=[jax.ShapeDtypeStruct((t_tok, N_MAIN), F32),
                   jax.ShapeDtypeStruct((t_tok, LANES), F32)],
        scratch_shapes=[pltpu.VMEM((tm, D_MODEL), BF16)],
        compiler_params=_cparams("parallel", "arbitrary"),
        name="in_projection",
    )(x2d, mod, w_main, w_lr)


def _ctx_attn_kernel(q_ref, k_ref, v_ref, z_ref, *rest):
    o_ref, ko_ref, vo_ref = rest[-3:]
    scale = NA_HEAD_DIM ** -0.5
    ko_ref[...] = k_ref[...]
    vo_ref[...] = v_ref[...]
    for h in range(NA_HEADS):
        cs = slice(h * NA_HEAD_DIM, (h + 1) * NA_HEAD_DIM)
        q = q_ref[:, cs].astype(BF16)
        k = k_ref[:, cs].astype(BF16)
        v = v_ref[:, cs].astype(BF16)
        s = _dot_nt(q, k) * scale
        e = jnp.exp(s - jnp.max(s, -1, keepdims=True))
        p = e / jnp.sum(e, -1, keepdims=True)
        o = _dot(p.astype(BF16), v)
        o_ref[:, cs] = (o * _silu(z_ref[:, cs])).astype(BF16)


def _context_attention(p_ctx, bsz, seq_len, layer, kv_prev):
    blk = lambda j: pl.BlockSpec((seq_len, NA_WIDTH), lambda b: (b, j))
    kv_spec = pl.BlockSpec((None, None, seq_len, NA_WIDTH), lambda b: (b, layer, 0, 0))
    kv_shape = jax.ShapeDtypeStruct((bsz, DEPTH, seq_len, NA_WIDTH), F32)
    return pl.pallas_call(
        _ctx_attn_kernel,
        grid=(bsz,),
        in_specs=[blk(COL_AQ // NA_WIDTH), blk(COL_AK // NA_WIDTH), blk(COL_AV // NA_WIDTH), blk(COL_AZ // NA_WIDTH)]
        + [pl.BlockSpec(memory_space=pl.ANY)] * len(kv_prev),
        out_specs=[pl.BlockSpec((seq_len, NA_WIDTH), lambda b: (b, 0)), kv_spec, kv_spec],
        out_shape=[jax.ShapeDtypeStruct((bsz * seq_len, NA_WIDTH), BF16), kv_shape, kv_shape],
        input_output_aliases={4 + i: 1 + i for i in range(len(kv_prev))},
        compiler_params=_cparams("parallel"),
        name="context_attention",
    )(p_ctx, p_ctx, p_ctx, p_ctx, *kv_prev)


NA_QROWS = 4
NA_KROWS = NA_QROWS + NA_WIN_H


NA_NDR = 2 * NA_WIN_H


def _na_plan(rows):
    kh = min(NA_WIN_H, rows)
    masked = NA_NDR - 1
    groups = []
    for r0 in range(0, rows, NA_QROWS):
        ks = int(np.clip(r0 - kh // 2, 0, rows - NA_KROWS))
        tiles = []
        for rq in range(NA_QROWS):
            r = r0 + rq
            rs = int(np.clip(r - kh // 2, 0, rows - kh))
            slot = [ks + i - r + NA_WIN_H - 1 if rs <= ks + i < rs + kh else masked for i in range(NA_KROWS)]
            row_tiles = []
            for i in range(0, NA_KROWS, 2):
                a, b = slot[i], slot[i + 1]
                if a != masked and b != masked:
                    row_tiles.append(b)
                elif b != masked:
                    row_tiles.append(NA_NDR + b)
                else:
                    row_tiles.append(2 * NA_NDR + a)
            tiles.append(tuple(row_tiles))
        groups.append((r0, ks, tuple(tiles)))
    return tuple(groups)


def _na_bias_tiles(rpb_l):
    nh = rpb_l.shape[0]
    n_dr = 2 * NA_WIN_H - 1
    c = np.arange(GRID_W)
    cs = np.clip(c - NA_WIN_W // 2, 0, GRID_W - NA_WIN_W)
    valid = (c[None, :] >= cs[:, None]) & (c[None, :] < cs[:, None] + NA_WIN_W)
    span = 2 * GRID_W
    left = (GRID_W - 1) - (NA_WIN_W - 1)
    wv = jnp.pad(rpb_l.astype(F32), ((0, 0), (0, 0), (left, span - left - (2 * NA_WIN_W - 1))))
    skew = jnp.tile(wv, (1, 1, GRID_W))[..., :GRID_W * (span - 1)].reshape(nh, n_dr, GRID_W, span - 1)
    toe = jnp.where(valid, skew[..., GRID_W - 1:], NEG_INF)
    neg1 = jnp.full((nh, 1, GRID_W, GRID_W), NEG_INF, F32)
    negs = jnp.full((nh, NA_NDR, GRID_W, GRID_W), NEG_INF, F32)
    cur = jnp.concatenate([toe, neg1], axis=1)
    prev = jnp.concatenate([neg1, toe], axis=1)
    return jnp.concatenate([jnp.concatenate([prev, cur], -1), jnp.concatenate([negs, cur], -1),
                            jnp.concatenate([cur, negs], -1)], axis=1)


def _na_kernel(q_ref, k_ref, v_ref, z_ref, kc_ref, vc_ref, bias_ref, o_ref, kb_scr, vb_scr, *, groups):
    scale = NA_HEAD_DIM ** -0.5
    kb_scr[...] = k_ref[...].astype(BF16)
    vb_scr[...] = v_ref[...].astype(BF16)
    kctx = kc_ref[...].astype(BF16)
    vctx = vc_ref[...].astype(BF16)
    nq, nk = NA_QROWS * GRID_W, NA_KROWS * GRID_W
    for r0, ks, tiles in groups:
        qsl = slice(r0 * GRID_W, r0 * GRID_W + nq)
        ksl = slice(ks * GRID_W, ks * GRID_W + nk)
        q = q_ref[qsl, :].astype(BF16)
        bias = jnp.concatenate([jnp.concatenate([bias_ref[t] for t in row], axis=1) for row in tiles], axis=0)
        s_win = _dot_nt(q, kb_scr[ksl, :]) * scale + bias
        s_ctx = _dot_nt(q, kctx) * scale
        m = jnp.maximum(jnp.max(s_win, -1, keepdims=True), jnp.max(s_ctx, -1, keepdims=True))
        e_win = jnp.exp(s_win - m)
        e_ctx = jnp.exp(s_ctx - m)
        den = jnp.sum(e_win, -1, keepdims=True) + jnp.sum(e_ctx, -1, keepdims=True)
        o = _dot((e_win / den).astype(BF16), vb_scr[ksl, :]) + _dot((e_ctx / den).astype(BF16), vctx)
        o_ref[qsl, :] = (o * _silu(z_ref[qsl, :])).astype(BF16)


def _neighbourhood_attention(p_lat, cache_k, cache_v, rpb_l, layer, bsz, n_tok):
    rows = n_tok // GRID_W
    assert rows % NA_QROWS == 0 and rows >= NA_KROWS and NA_KROWS % 2 == 0
    groups = _na_plan(rows)
    bias_tab = _na_bias_tiles(rpb_l)
    past = cache_k.shape[2]
    hd = NA_HEAD_DIM
    blk = lambda col: pl.BlockSpec((n_tok, hd), lambda b, h: (b, col // hd + h))
    cblk = pl.BlockSpec((None, None, past, hd), lambda b, h: (b, layer, 0, h))
    return pl.pallas_call(
        functools.partial(_na_kernel, groups=groups),
        grid=(bsz, NA_HEADS),
        in_specs=[blk(COL_AQ), blk(COL_AK), blk(COL_AV), blk(COL_AZ), cblk, cblk,
                  pl.BlockSpec((None,) + bias_tab.shape[1:], lambda b, h: (h, 0, 0, 0))],
        out_specs=pl.BlockSpec((n_tok, hd), lambda b, h: (b, h)),
        out_shape=jax.ShapeDtypeStruct((bsz * n_tok, NA_WIDTH), BF16),
        scratch_shapes=[pltpu.VMEM((n_tok, hd), BF16), pltpu.VMEM((n_tok, hd), BF16)],
        compiler_params=_cparams("parallel", "parallel"),
        name="neighbourhood_attention",
    )(p_lat, p_lat, p_lat, p_lat, cache_k, cache_v, bias_tab)


def _rope_tables(seq_len):
    t = jnp.arange(seq_len)
    half = GLA_DK // 2
    inv = ROPE_BASE ** (-jnp.arange(0, half, 2, dtype=F32) / half)

    def tab(pos):
        ang = pos.astype(F32)[:, None] * inv[None, :]
        return jnp.cos(ang), jnp.sin(ang)

    cr, sr = tab(t // GRID_W)
    cc, sc = tab(t % GRID_W)
    cos = jnp.concatenate([cr, cr, cc, cc], -1)
    sin = jnp.concatenate([-sr, sr, -sc, sc], -1)
    return cos, sin


def _rope(x, cos, sin):
    quarter = GLA_DK // 4
    lane = lax.broadcasted_iota(jnp.int32, x.shape, 1)
    partner = jnp.where((lane % (2 * quarter)) < quarter,
                        pltpu.roll(x, GLA_DK - quarter, 1), pltpu.roll(x, quarter, 1))
    return x * cos + partner * sin


def _log_sigmoid(x):
    return jnp.minimum(x, 0.0) - jnp.log1p(jnp.exp(-jnp.abs(x)))


def _chunk_scan(x, reverse):
    n = x.shape[0]
    pos = lax.broadcasted_iota(jnp.int32, x.shape, 0) % GLA_CHUNK
    step = 1
    while step < GLA_CHUNK:
        if reverse:
            x = x + jnp.where(pos < GLA_CHUNK - step, pltpu.roll(x, n - step, 0), 0.0)
        else:
            x = x + jnp.where(pos >= step, pltpu.roll(x, step, 0), 0.0)
        step *= 2
    return x


GLA_GROUP = 4


def _gla_kernel(*refs, latent, n_groups, heads):
    (q_ref, k_ref, v_ref, z_ref, lr_ref, wgf_ref, wgb_ref, bg_ref, ng_ref), rest = refs[:9], refs[9:]
    if latent:
        cos_ref, sin_ref, s0_ref, o_ref, qs, ks, laf, lab, o_f, o_b, st_f, st_b = rest
    else:
        o_ref, sfin_ref, qs, ks, laf, lab, o_f, o_b, st_f, st_b = rest[-10:]
    c_len = GLA_CHUNK
    g_len = GLA_GROUP * c_len
    dk, dv = GLA_DK, GLA_DV

    q = q_ref[...] * (GLA_DK ** -0.5)
    k = k_ref[...]
    if latent:
        q = _rope(q, cos_ref[...], sin_ref[...])
        k = _rope(k, cos_ref[...], sin_ref[...])
    qs[...] = q
    ks[...] = k
    lrb = lr_ref[...].astype(BF16)
    laf[...] = _log_sigmoid(_dot(lrb, wgf_ref[...]) + bg_ref[0:1, :]) * (1.0 / GLA_TAU)
    lab[...] = _log_sigmoid(_dot(lrb, wgb_ref[...]) + bg_ref[1:2, :]) * (1.0 / GLA_TAU)

    ii = lax.broadcasted_iota(jnp.int32, (c_len, c_len), 0)
    jj = lax.broadcasted_iota(jnp.int32, (c_len, c_len), 1)

    def half_group(g, reverse, hh):
        la_ref, out, st = (lab, o_b, st_b) if reverse else (laf, o_f, st_f)
        kcols, vcols, srows = slice(hh * dk, (hh + 1) * dk), slice(hh * dv, (hh + 1) * dv), slice(hh * dv, (hh + 1) * dv)
        mask = (ii <= jj) if reverse else (ii >= jj)
        edge = 0 if reverse else c_len - 1
        start = g * g_len if isinstance(g, int) else pl.multiple_of(g * g_len, g_len)
        b_all = _chunk_scan(la_ref[pl.ds(start, g_len), kcols], reverse)
        s_t = st[srows, :]
        for c in (range(GLA_GROUP - 1, -1, -1) if reverse else range(GLA_GROUP)):
            sl = pl.ds(start + c * c_len, c_len)
            b = b_all[c * c_len:(c + 1) * c_len]
            btot = b[edge:edge + 1, :]
            qc = qs[sl, kcols]
            kc = ks[sl, kcols]
            vc = v_ref[sl, vcols]
            qe = (qc * jnp.exp(b)).astype(BF16)
            ke = (kc * jnp.exp(-b)).astype(BF16)
            kw = (kc * jnp.exp(btot - b)).astype(BF16)
            att = jnp.where(mask, _dot_nt(qe, ke), 0.0)
            out[sl, vcols] = _dot(att.astype(BF16), vc.astype(BF16)) + _dot_nt(qe, s_t.astype(BF16))
            s_t = jnp.exp(btot) * s_t + _dot(vc.T.astype(BF16), kw)
        st[srows, :] = s_t

    for direction, st in enumerate((st_f, st_b)):
        st[...] = s0_ref[direction].T if latent else jnp.zeros_like(st)
    if n_groups == 1:
        for hh in range(heads):
            half_group(0, False, hh)
            half_group(0, True, hh)
    else:
        def body(i, carry):
            for hh in range(heads):
                half_group(i, False, hh)
                half_group(n_groups - 1 - i, True, hh)
            return carry
        lax.fori_loop(0, n_groups, body, 0)

    for hh in range(heads):
        kcols, vcols = slice(hh * dk, (hh + 1) * dk), slice(hh * dv, (hh + 1) * dv)
        if not latent:
            sfin_ref[0, hh] = st_f[vcols, :].T
            sfin_ref[1, hh] = st_b[vcols, :].T
        o = (o_f[:, vcols] + o_b[:, vcols]) - jnp.sum(qs[:, kcols] * ks[:, kcols], -1, keepdims=True) * v_ref[:, vcols]
        o = o * lax.rsqrt(jnp.mean(o * o, -1, keepdims=True) + LN_EPS) * ng_ref[:, vcols]
        o_ref[:, vcols] = (o * _silu(z_ref[:, vcols])).astype(BF16)


def _gla(p, p_lr, wg_f, wg_b, b_gate, norm_g, bsz, seq_len, layer, rope=None, state=None, state_prev=()):
    latent = state is not None
    assert seq_len % (GLA_GROUP * GLA_CHUNK) == 0
    dk, dv = GLA_DK, GLA_DV
    hps = 1 if latent else GLA_HEADS
    kw, vw = hps * dk, hps * dv
    in_specs = [pl.BlockSpec((seq_len, kw), lambda b, h: (b, COL_BQ // kw + h)),
                pl.BlockSpec((seq_len, kw), lambda b, h: (b, COL_BK // kw + h)),
                pl.BlockSpec((seq_len, vw), lambda b, h: (b, COL_BV // vw + h)),
                pl.BlockSpec((seq_len, vw), lambda b, h: (b, COL_BZ // vw + h)),
                pl.BlockSpec((seq_len, LANES), lambda b, h: (b, 0)),
                pl.BlockSpec((LANES, kw), lambda b, h: (0, h)),
                pl.BlockSpec((LANES, kw), lambda b, h: (0, h)),
                pl.BlockSpec((2, kw), lambda b, h: (0, h)),
                pl.BlockSpec((1, vw), lambda b, h: (0, h))]
    args = [p, p, p, p, p_lr, wg_f, wg_b, b_gate, norm_g]
    o_spec = pl.BlockSpec((seq_len, vw), lambda b, h: (b, h))
    o_shape = jax.ShapeDtypeStruct((bsz * seq_len, GLA_VW), BF16)
    if latent:
        cos, sin = rope
        in_specs += [pl.BlockSpec((seq_len, dk), lambda b, h: (0, 0)),
                     pl.BlockSpec((seq_len, dk), lambda b, h: (0, 0)),
                     pl.BlockSpec((None, None, 2, None, dk, dv), lambda b, h: (b, layer, 0, h, 0, 0))]
        args += [cos, sin, state]
        out_specs, out_shape, aliases = o_spec, o_shape, {}
    else:
        aliases = {len(args) + i: 1 + i for i in range(len(state_prev))}
        in_specs += [pl.BlockSpec(memory_space=pl.ANY)] * len(state_prev)
        args += list(state_prev)
        out_specs = [o_spec, pl.BlockSpec((None, None, 2, hps, dk, dv), lambda b, h: (b, layer, 0, h, 0, 0))]
        out_shape = [o_shape, jax.ShapeDtypeStruct((bsz, DEPTH, 2, GLA_HEADS, dk, dv), F32)]
    return pl.pallas_call(
        functools.partial(_gla_kernel, latent=latent, n_groups=seq_len // (GLA_GROUP * GLA_CHUNK), heads=hps),
        grid=(bsz, GLA_HEADS // hps),
        in_specs=in_specs,
        out_specs=out_specs,
        out_shape=out_shape,
        input_output_aliases=aliases,
        scratch_shapes=[pltpu.VMEM((seq_len, kw), F32), pltpu.VMEM((seq_len, kw), F32),
                        pltpu.VMEM((seq_len, kw), F32), pltpu.VMEM((seq_len, kw), F32),
                        pltpu.VMEM((seq_len, vw), F32), pltpu.VMEM((seq_len, vw), F32),
                        pltpu.VMEM((vw, dk), F32), pltpu.VMEM((vw, dk), F32)],
        compiler_params=_cparams("parallel", "parallel"),
        name="gla_latent" if latent else "gla_context",
    )(*args)


def _conv_merge_kernel(u_ref, g_ref, up_ref, gp_ref, un_ref, gn_ref, z_ref, w_ref, b_ref, ng_ref, nb_ref,
                       oa_ref, ob_ref, ga_ref, gb_ref, gc_ref, wa_ref, wb_ref, wc_ref, o_ref, pad, cv,
                       *, tiles_per_seq, tt):
    m = jax.nn.sigmoid(ga_ref[...]) * _dot(oa_ref[...], wa_ref[...])
    m = m + jax.nn.sigmoid(gb_ref[...]) * _dot(ob_ref[...], wb_ref[...])
    t = pl.program_id(0) % tiles_per_seq
    halo = CONV_HALO
    glu = lambda u, g: u * jax.nn.sigmoid(g)
    pad[halo:halo + tt, :] = glu(u_ref[...], g_ref[...])
    pad[0:halo, :] = jnp.where(t > 0, glu(up_ref[...], gp_ref[...]), 0.0)
    pad[halo + tt:halo + tt + halo, :] = jnp.where(t < tiles_per_seq - 1, glu(un_ref[...], gn_ref[...]), 0.0)
    first = halo - CONV_TAPS // 2
    rb = 64
    reach = -(-(first + CONV_TAPS - 1) // SUBLANES) * SUBLANES
    for c0 in range(0, CONV_WIDTH, LANES):
        lanes = slice(c0, c0 + LANES)
        w_strip = w_ref[:, lanes]
        for r0 in range(0, tt, rb):
            base = pad[r0:r0 + rb + reach, lanes]
            acc = jnp.broadcast_to(b_ref[:, lanes], (rb, LANES))
            for phase in range(SUBLANES):
                rows = pltpu.roll(base, rb + reach - phase, 0) if phase else base
                for a in range(0, reach, SUBLANES):
                    j = a + phase - first
                    if 0 <= j < CONV_TAPS:
                        acc = acc + rows[a:a + rb] * w_strip[j:j + 1, :]
            cv[r0:r0 + rb, lanes] = acc
    acc = cv[...]
    mu = jnp.mean(acc, -1, keepdims=True)
    xc = acc - mu
    var = jnp.mean(xc * xc, -1, keepdims=True)
    y = xc * lax.rsqrt(var + LN_EPS) * ng_ref[...] + nb_ref[...]
    oc = (_silu(y) * _silu(z_ref[...])).astype(BF16)
    m = m + jax.nn.sigmoid(gc_ref[...]) * _dot(oc, wc_ref[...])
    o_ref[...] = m.astype(BF16)


def _conv_merge(p, og_a, og_b, conv_w, conv_b, norm_g, norm_b, wa, wb, wc, seq_len):
    t_tok = p.shape[0]
    tt = 256
    tiles_per_seq = seq_len // tt
    w = CONV_WIDTH
    d = D_MODEL
    hb = tt // CONV_HALO
    n_halo = t_tok // CONV_HALO
    main = lambda col: pl.BlockSpec((tt, w), lambda i: (i, col // w))
    prev = lambda col: pl.BlockSpec((CONV_HALO, w), lambda i: (jnp.maximum(i * hb - 1, 0), col // w))
    nxt = lambda col: pl.BlockSpec((CONV_HALO, w), lambda i: (jnp.minimum((i + 1) * hb, n_halo - 1), col // w))
    vec = pl.BlockSpec((1, w), lambda i: (0, 0))
    br = pl.BlockSpec((tt, w), lambda i: (i, 0))
    gate = lambda col: pl.BlockSpec((tt, d), lambda i: (i, col // d))
    wspec = pl.BlockSpec((w, d), lambda i: (0, 0), pipeline_mode=pl.Buffered(1))
    return pl.pallas_call(
        functools.partial(_conv_merge_kernel, tiles_per_seq=tiles_per_seq, tt=tt),
        grid=(t_tok // tt,),
        in_specs=[main(COL_CU), main(COL_CG), prev(COL_CU), prev(COL_CG), nxt(COL_CU), nxt(COL_CG), main(COL_CZ),
                  pl.BlockSpec((CONV_TAPS, w), lambda i: (0, 0)), vec, vec, vec,
                  br, br, gate(COL_GA), gate(COL_GB), gate(COL_GC), wspec, wspec, wspec],
        out_specs=pl.BlockSpec((tt, d), lambda i: (i, 0)),
        out_shape=jax.ShapeDtypeStruct((t_tok, d), BF16),
        scratch_shapes=[pltpu.VMEM((tt + 2 * CONV_HALO, w), F32), pltpu.VMEM((tt, w), F32)],
        compiler_params=_cparams("parallel"),
        name="conv_merge",
    )(p, p, p, p, p, p, p, conv_w, conv_b, norm_g, norm_b, og_a, og_b, p, p, p, wa, wb, wc)


OUT_SUB = 256


def _out_kernel(m_ref, w_ref, x_ref, mod_ref, g_ref, b_ref, o_ref):
    gate = mod_ref[0, :, 2 * D_MODEL:3 * D_MODEL]
    for r0 in range(0, m_ref.shape[0], OUT_SUB):
        rows = slice(r0, r0 + OUT_SUB)
        y = ALPHA * x_ref[rows, :] + gate * _dot(m_ref[rows, :], w_ref[...])
        mu = jnp.mean(y, -1, keepdims=True)
        yc = y - mu
        var = jnp.mean(yc * yc, -1, keepdims=True)
        o_ref[rows, :] = yc * lax.rsqrt(var + LN_EPS) * g_ref[...] + b_ref[...]


def _out_projection(merged, w_out, x2d, mod, ln_g, ln_b, seq_len):
    t_tok = x2d.shape[0]
    tm = 2 * OUT_SUB
    d = D_MODEL
    per = seq_len // tm if mod.shape[0] > 1 else t_tok // tm
    row = pl.BlockSpec((tm, d), lambda i: (i, 0))
    vec = pl.BlockSpec((1, d), lambda i: (0, 0))
    return pl.pallas_call(
        _out_kernel,
        grid=(t_tok // tm,),
        in_specs=[row, pl.BlockSpec((d, d), lambda i: (0, 0), pipeline_mode=pl.Buffered(1)), row,
                  pl.BlockSpec((1, 1, 3 * d), lambda i: (i // per, 0, 0)), vec, vec],
        out_specs=row,
        out_shape=jax.ShapeDtypeStruct((t_tok, d), F32),
        compiler_params=_cparams("parallel"),
        name="out_projection",
    )(merged, w_out, x2d, mod, ln_g, ln_b)


def kernel(x_prompt, x_sample, cache_k, cache_v, state_gla, c, c_ctx, w_mod, b_mod, w_in, rpb, gla_w_gate,
           gla_b_gate, gla_norm_g, conv_w, conv_b, conv_norm_g, conv_norm_b, w_proj_a, w_proj_b, w_proj_c,
           w_out, ln_g, ln_b):
    bsz, seq_len, d = x_prompt.shape
    dbsz, dseq, _ = x_sample.shape
    past = cache_k.shape[2]

    cvec = jnp.zeros((8, d), F32).at[0].set(c_ctx).at[1:1 + dbsz].set(c)
    mod_all = _modulation(cvec, w_mod, b_mod)

    w_main, w_lr = _regroup_w_in(w_in)
    wg = gla_w_gate.astype(BF16)
    wg_f = jnp.pad(wg[:, 0], ((0, 0), (0, LANES - GLA_LOWRANK), (0, 0)))
    wg_b = jnp.pad(wg[:, 1], ((0, 0), (GLA_LOWRANK, LANES - 2 * GLA_LOWRANK), (0, 0)))
    wpa, wpb, wpc, wo = (w.astype(BF16) for w in (w_proj_a, w_proj_b, w_proj_c, w_out))
    ck = cache_k.reshape(dbsz, DEPTH, past, NA_WIDTH)
    cv = cache_v.reshape(dbsz, DEPTH, past, NA_WIDTH)
    rope = _rope_tables(dseq)

    h_ctx = x_prompt.reshape(bsz * seq_len, d)
    h_lat = x_sample.reshape(dbsz * dseq, d)
    kv_new, s_new = (), ()
    for l in range(DEPTH):
        mod_ctx = mod_all[l, 0:1][:, None, :]
        mod_lat = mod_all[l, 1:1 + dbsz][:, None, :]
        row = lambda a: a[l][None, :]
        gla_w = (wg_f[l], wg_b[l], gla_b_gate[l], row(gla_norm_g))
        conv_p = (conv_w[l], row(conv_b), row(conv_norm_g), row(conv_norm_b))

        def tail(x2d, p, og_a, og_b, mod, seq):
            merged = _conv_merge(p, og_a, og_b, *conv_p, wpa[l], wpb[l], wpc[l], seq)
            return _out_projection(merged, wo[l], x2d, mod, row(ln_g), row(ln_b), seq)

        p, p_lr = _in_projection(h_ctx, mod_ctx, w_main[l], w_lr[l], seq_len)
        og_a, *kv_new = _context_attention(p, bsz, seq_len, l, kv_new)
        og_b, *s_new = _gla(p, p_lr, *gla_w, bsz, seq_len, l, state_prev=s_new)
        h_ctx = tail(h_ctx, p, og_a, og_b, mod_ctx, seq_len)

        p, p_lr = _in_projection(h_lat, mod_lat, w_main[l], w_lr[l], dseq)
        og_a = _neighbourhood_attention(p, ck, cv, rpb[l], l, dbsz, dseq)
        og_b = _gla(p, p_lr, *gla_w, dbsz, dseq, l, rope=rope, state=state_gla)
        h_lat = tail(h_lat, p, og_a, og_b, mod_lat, dseq)

    new_k, new_v = (a.reshape(bsz, DEPTH, seq_len, NA_HEADS, NA_HEAD_DIM) for a in kv_new)
    return (h_ctx.reshape(bsz, seq_len, d), h_lat.reshape(dbsz, dseq, d), new_k, new_v, s_new[0])
```

```python
import functools

import numpy as np
import jax
import jax.numpy as jnp
from jax import lax
from jax.experimental import pallas as pl
from jax.experimental.pallas import tpu as pltpu

D_MODEL = 2048
DEPTH = 2
GRID_W = 64
NA_HEADS = 8
NA_HEAD_DIM = 128
NA_WIDTH = NA_HEADS * NA_HEAD_DIM
NA_WIN_H = 8
NA_WIN_W = 16
GLA_HEADS = 4
GLA_DK = 128
GLA_DV = 256
GLA_KW = GLA_HEADS * GLA_DK
GLA_VW = GLA_HEADS * GLA_DV
GLA_LOWRANK = 16
GLA_TAU = 16.0
GLA_CHUNK = 64
ROPE_BASE = 10000.0
CONV_WIDTH = 1024
CONV_TAPS = 31
ALPHA = (2 * DEPTH) ** 0.25
LN_EPS = 1e-5
NEG_INF = -1e30

LANES = 128
SUBLANES = 8
BF16_ROWS = 2 * SUBLANES
CONV_HALO = 16
VMEM_LIMIT = 56 * 1024 * 1024

COL_AQ, COL_AK, COL_AV, COL_AZ = 0, 1024, 2048, 3072
COL_BQ, COL_BK, COL_BV, COL_BZ = 4096, 4608, 5120, 6144
COL_CU, COL_CG, COL_CZ = 7168, 8192, 9216
COL_GA, COL_GB, COL_GC = 10240, 12288, 14336
N_MAIN = 16384
LR_START = 6144

BF16 = jnp.bfloat16
F32 = jnp.float32


def _cparams(*sem):
    return pltpu.CompilerParams(dimension_semantics=sem, vmem_limit_bytes=VMEM_LIMIT)


def _silu(x):
    return x * jax.nn.sigmoid(x)


def _dot(a, b):
    return jnp.dot(a, b, preferred_element_type=F32)


def _dot_nt(a, b):
    return lax.dot_general(a, b, (((1,), (1,)), ((), ())), preferred_element_type=F32)


def _mod_kernel(cv_ref, w_ref, b_ref, o_ref):
    s = _silu(cv_ref[...]).astype(BF16)
    o_ref[0] = _dot(s, w_ref[0].astype(BF16)) + b_ref[0]


def _modulation(cvec, w_mod, b_mod):
    tn = 1024
    n3 = 3 * D_MODEL
    return pl.pallas_call(
        _mod_kernel,
        grid=(DEPTH, n3 // tn),
        in_specs=[pl.BlockSpec((8, D_MODEL), lambda l, n: (0, 0)),
                  pl.BlockSpec((1, D_MODEL, tn), lambda l, n: (l, 0, n)),
                  pl.BlockSpec((1, 1, tn), lambda l, n: (l, 0, n))],
        out_specs=pl.BlockSpec((1, 8, tn), lambda l, n: (l, 0, n)),
        out_shape=jax.ShapeDtypeStruct((DEPTH, 8, n3), F32),
        compiler_params=_cparams("parallel", "parallel"),
        name="modulation",
    )(cvec, w_mod, b_mod.reshape(DEPTH, 1, n3))


def _cast_kernel(a_ref, o_ref):
    o_ref[...] = a_ref[...].astype(BF16)


def _cast_rows_bf16(w_t):
    depth, rows, d = w_t.shape
    tr = max(t for t in range(BF16_ROWS, 1024 + 1, BF16_ROWS) if rows % t == 0)
    spec = pl.BlockSpec((1, tr, d), lambda l, r: (l, r, 0))
    return pl.pallas_call(
        _cast_kernel,
        grid=(depth, rows // tr),
        in_specs=[spec],
        out_specs=spec,
        out_shape=jax.ShapeDtypeStruct(w_t.shape, BF16),
        compiler_params=_cparams("parallel", "parallel"),
        name="cast_w_in",
    )(w_t)


def _inproj_kernel(x_ref, mod_ref, w_ref, wlr_ref, o_ref, olr_ref, h_scr):
    strip = 256

    @pl.when(pl.program_id(1) == 0)
    def _():
        shift = mod_ref[0, :, 0:D_MODEL]
        scale = mod_ref[0, :, D_MODEL:2 * D_MODEL]

        def body(i, carry):
            sl = pl.ds(pl.multiple_of(i * strip, strip), strip)
            x = x_ref[sl, :]
            mu = jnp.mean(x, -1, keepdims=True)
            xc = x - mu
            var = jnp.mean(xc * xc, -1, keepdims=True)
            y = xc * lax.rsqrt(var + LN_EPS)
            hb = (y * (1.0 + scale) + shift).astype(BF16)
            h_scr[sl, :] = hb
            olr_ref[sl, :] = _dot_nt(hb, wlr_ref[...])
            return carry

        lax.fori_loop(0, x_ref.shape[0] // strip, body, 0)

    o_ref[...] = _dot_nt(h_scr[...], w_ref[0])


def _in_projection(x2d, mod, w_t, w_lr_t, layer, seq_len):
    t_tok = x2d.shape[0]
    tm, tn = 1024, 1024
    assert LR_START % tn == 0
    per = seq_len // tm if mod.shape[0] > 1 else t_tok // tm
    w_row = lambda n: pl.multiple_of(n * tn + jnp.where(n >= LR_START // tn, 2 * GLA_LOWRANK, 0), 2 * GLA_LOWRANK)
    return pl.pallas_call(
        _inproj_kernel,
        grid=(t_tok // tm, N_MAIN // tn),
        in_specs=[pl.BlockSpec((tm, D_MODEL), lambda m, n: (m, 0)),
                  pl.BlockSpec((1, 1, 3 * D_MODEL), lambda m, n: (m // per, 0, 0)),
                  pl.BlockSpec((pl.Element(1), pl.Element(tn), pl.Element(D_MODEL)),
                               lambda m, n: (layer, w_row(n), 0)),
                  pl.BlockSpec((None, LANES, D_MODEL), lambda m, n: (layer, 0, 0))],
        out_specs=[pl.BlockSpec((tm, tn), lambda m, n: (m, n)),
                   pl.BlockSpec((tm, LANES), lambda m, n: (m, 0))],
        out_shape=[jax.ShapeDtypeStruct((t_tok, N_MAIN), F32),
                   jax.ShapeDtypeStruct((t_tok, LANES), F32)],
        scratch_shapes=[pltpu.VMEM((tm, D_MODEL), BF16)],
        compiler_params=_cparams("parallel", "arbitrary"),
        name="in_projection",
    )(x2d, mod, w_t, w_lr_t)


def _ctx_attn_kernel(q_ref, k_ref, v_ref, z_ref, *rest):
    o_ref, ko_ref, vo_ref = rest[-3:]
    scale = NA_HEAD_DIM ** -0.5
    ko_ref[...] = k_ref[...]
    vo_ref[...] = v_ref[...]
    for h in range(NA_HEADS):
        cs = slice(h * NA_HEAD_DIM, (h + 1) * NA_HEAD_DIM)
        q = q_ref[:, cs].astype(BF16)
        k = k_ref[:, cs].astype(BF16)
        v = v_ref[:, cs].astype(BF16)
        s = _dot_nt(q, k) * scale
        e = jnp.exp(s - jnp.max(s, -1, keepdims=True))
        p = e / jnp.sum(e, -1, keepdims=True)
        o = _dot(p.astype(BF16), v)
        o_ref[:, cs] = (o * _silu(z_ref[:, cs])).astype(BF16)


def _context_attention(p_ctx, bsz, seq_len, layer, kv_prev):
    blk = lambda j: pl.BlockSpec((seq_len, NA_WIDTH), lambda b: (b, j))
    kv_spec = pl.BlockSpec((None, None, seq_len, NA_WIDTH), lambda b: (b, layer, 0, 0))
    kv_shape = jax.ShapeDtypeStruct((bsz, DEPTH, seq_len, NA_WIDTH), F32)
    return pl.pallas_call(
        _ctx_attn_kernel,
        grid=(bsz,),
        in_specs=[blk(COL_AQ // NA_WIDTH), blk(COL_AK // NA_WIDTH), blk(COL_AV // NA_WIDTH), blk(COL_AZ // NA_WIDTH)]
        + [pl.BlockSpec(memory_space=pl.ANY)] * len(kv_prev),
        out_specs=[pl.BlockSpec((seq_len, NA_WIDTH), lambda b: (b, 0)), kv_spec, kv_spec],
        out_shape=[jax.ShapeDtypeStruct((bsz * seq_len, NA_WIDTH), BF16), kv_shape, kv_shape],
        input_output_aliases={4 + i: 1 + i for i in range(len(kv_prev))},
        compiler_params=_cparams("parallel"),
        name="context_attention",
    )(p_ctx, p_ctx, p_ctx, p_ctx, *kv_prev)


NA_QROWS = 4
NA_KROWS = NA_QROWS + NA_WIN_H


NA_NDR = 2 * NA_WIN_H


def _na_plan(rows):
    kh = min(NA_WIN_H, rows)
    masked = NA_NDR - 1
    groups = []
    for r0 in range(0, rows, NA_QROWS):
        ks = int(np.clip(r0 - kh // 2, 0, rows - NA_KROWS))
        tiles = []
        for rq in range(NA_QROWS):
            r = r0 + rq
            rs = int(np.clip(r - kh // 2, 0, rows - kh))
            slot = [ks + i - r + NA_WIN_H - 1 if rs <= ks + i < rs + kh else masked for i in range(NA_KROWS)]
            row_tiles = []
            for i in range(0, NA_KROWS, 2):
                a, b = slot[i], slot[i + 1]
                if a != masked and b != masked:
                    row_tiles.append(b)
                elif b != masked:
                    row_tiles.append(NA_NDR + b)
                else:
                    row_tiles.append(2 * NA_NDR + a)
            tiles.append(tuple(row_tiles))
        groups.append((r0, ks, tuple(tiles)))
    return tuple(groups)


def _na_bias_tiles(rpb_l):
    nh = rpb_l.shape[0]
    n_dr = 2 * NA_WIN_H - 1
    c = np.arange(GRID_W)
    cs = np.clip(c - NA_WIN_W // 2, 0, GRID_W - NA_WIN_W)
    valid = (c[None, :] >= cs[:, None]) & (c[None, :] < cs[:, None] + NA_WIN_W)
    span = 2 * GRID_W
    left = (GRID_W - 1) - (NA_WIN_W - 1)
    wv = jnp.pad(rpb_l.astype(F32), ((0, 0), (0, 0), (left, span - left - (2 * NA_WIN_W - 1))))
    skew = jnp.tile(wv, (1, 1, GRID_W))[..., :GRID_W * (span - 1)].reshape(nh, n_dr, GRID_W, span - 1)
    toe = jnp.where(valid, skew[..., GRID_W - 1:], NEG_INF)
    neg1 = jnp.full((nh, 1, GRID_W, GRID_W), NEG_INF, F32)
    negs = jnp.full((nh, NA_NDR, GRID_W, GRID_W), NEG_INF, F32)
    cur = jnp.concatenate([toe, neg1], axis=1)
    prev = jnp.concatenate([neg1, toe], axis=1)
    return jnp.concatenate([jnp.concatenate([prev, cur], -1), jnp.concatenate([negs, cur], -1),
                            jnp.concatenate([cur, negs], -1)], axis=1)


def _na_kernel(q_ref, k_ref, v_ref, z_ref, kc_ref, vc_ref, bias_ref, o_ref, kb_scr, vb_scr, *, groups):
    scale = NA_HEAD_DIM ** -0.5
    kb_scr[...] = k_ref[...].astype(BF16)
    vb_scr[...] = v_ref[...].astype(BF16)
    kctx = kc_ref[...].astype(BF16)
    vctx = vc_ref[...].astype(BF16)
    nq, nk = NA_QROWS * GRID_W, NA_KROWS * GRID_W
    for r0, ks, tiles in groups:
        qsl = slice(r0 * GRID_W, r0 * GRID_W + nq)
        ksl = slice(ks * GRID_W, ks * GRID_W + nk)
        q = q_ref[qsl, :].astype(BF16)
        bias = jnp.concatenate([jnp.concatenate([bias_ref[t] for t in row], axis=1) for row in tiles], axis=0)
        s_win = _dot_nt(q, kb_scr[ksl, :]) * scale + bias
        s_ctx = _dot_nt(q, kctx) * scale
        m = jnp.maximum(jnp.max(s_win, -1, keepdims=True), jnp.max(s_ctx, -1, keepdims=True))
        e_win = jnp.exp(s_win - m)
        e_ctx = jnp.exp(s_ctx - m)
        den = jnp.sum(e_win, -1, keepdims=True) + jnp.sum(e_ctx, -1, keepdims=True)
        o = _dot((e_win / den).astype(BF16), vb_scr[ksl, :]) + _dot((e_ctx / den).astype(BF16), vctx)
        o_ref[qsl, :] = (o * _silu(z_ref[qsl, :])).astype(BF16)


def _neighbourhood_attention(p_lat, cache_k, cache_v, rpb_l, layer, bsz, n_tok):
    rows = n_tok // GRID_W
    assert rows % NA_QROWS == 0 and rows >= NA_KROWS and NA_KROWS % 2 == 0
    groups = _na_plan(rows)
    bias_tab = _na_bias_tiles(rpb_l)
    past = cache_k.shape[2]
    hd = NA_HEAD_DIM
    blk = lambda col: pl.BlockSpec((n_tok, hd), lambda b, h: (b, col // hd + h))
    cblk = pl.BlockSpec((None, None, past, hd), lambda b, h: (b, layer, 0, h))
    return pl.pallas_call(
        functools.partial(_na_kernel, groups=groups),
        grid=(bsz, NA_HEADS),
        in_specs=[blk(COL_AQ), blk(COL_AK), blk(COL_AV), blk(COL_AZ), cblk, cblk,
                  pl.BlockSpec((None,) + bias_tab.shape[1:], lambda b, h: (h, 0, 0, 0))],
        out_specs=pl.BlockSpec((n_tok, hd), lambda b, h: (b, h)),
        out_shape=jax.ShapeDtypeStruct((bsz * n_tok, NA_WIDTH), BF16),
        scratch_shapes=[pltpu.VMEM((n_tok, hd), BF16), pltpu.VMEM((n_tok, hd), BF16)],
        compiler_params=_cparams("parallel", "parallel"),
        name="neighbourhood_attention",
    )(p_lat, p_lat, p_lat, p_lat, cache_k, cache_v, bias_tab)


def _rope_tables(seq_len):
    t = jnp.arange(seq_len)
    half = GLA_DK // 2
    inv = ROPE_BASE ** (-jnp.arange(0, half, 2, dtype=F32) / half)

    def tab(pos):
        ang = pos.astype(F32)[:, None] * inv[None, :]
        return jnp.cos(ang), jnp.sin(ang)

    cr, sr = tab(t // GRID_W)
    cc, sc = tab(t % GRID_W)
    cos = jnp.concatenate([cr, cr, cc, cc], -1)
    sin = jnp.concatenate([-sr, sr, -sc, sc], -1)
    return cos, sin


def _rope(x, cos, sin):
    quarter = GLA_DK // 4
    lane = lax.broadcasted_iota(jnp.int32, x.shape, 1)
    partner = jnp.where((lane % (2 * quarter)) < quarter,
                        pltpu.roll(x, GLA_DK - quarter, 1), pltpu.roll(x, quarter, 1))
    return x * cos + partner * sin


def _log_sigmoid(x):
    return jnp.minimum(x, 0.0) - jnp.log1p(jnp.exp(-jnp.abs(x)))


def _chunk_scan(x, reverse):
    n = x.shape[0]
    pos = lax.broadcasted_iota(jnp.int32, x.shape, 0) % GLA_CHUNK
    step = 1
    while step < GLA_CHUNK:
        if reverse:
            x = x + jnp.where(pos < GLA_CHUNK - step, pltpu.roll(x, n - step, 0), 0.0)
        else:
            x = x + jnp.where(pos >= step, pltpu.roll(x, step, 0), 0.0)
        step *= 2
    return x


GLA_GROUP = 4


def _gla_kernel(*refs, latent, n_groups, heads):
    (q_ref, k_ref, v_ref, z_ref, lr_ref, wgf_ref, wgb_ref, bg_ref, ng_ref), rest = refs[:9], refs[9:]
    if latent:
        cos_ref, sin_ref, s0_ref, o_ref, qs, ks, laf, lab, o_f, o_b, st_f, st_b = rest
    else:
        o_ref, sfin_ref, qs, ks, laf, lab, o_f, o_b, st_f, st_b = rest[-10:]
    c_len = GLA_CHUNK
    g_len = GLA_GROUP * c_len
    dk, dv = GLA_DK, GLA_DV

    q = q_ref[...] * (GLA_DK ** -0.5)
    k = k_ref[...]
    if latent:
        q = _rope(q, cos_ref[...], sin_ref[...])
        k = _rope(k, cos_ref[...], sin_ref[...])
    qs[...] = q
    ks[...] = k
    lrb = lr_ref[...].astype(BF16)
    laf[...] = _log_sigmoid(_dot(lrb, wgf_ref[...]) + bg_ref[0:1, :]) * (1.0 / GLA_TAU)
    lab[...] = _log_sigmoid(_dot(lrb, wgb_ref[...]) + bg_ref[1:2, :]) * (1.0 / GLA_TAU)

    ii = lax.broadcasted_iota(jnp.int32, (c_len, c_len), 0)
    jj = lax.broadcasted_iota(jnp.int32, (c_len, c_len), 1)

    def half_group(g, reverse, hh):
        la_ref, out, st = (lab, o_b, st_b) if reverse else (laf, o_f, st_f)
        kcols, vcols, srows = slice(hh * dk, (hh + 1) * dk), slice(hh * dv, (hh + 1) * dv), slice(hh * dv, (hh + 1) * dv)
        mask = (ii <= jj) if reverse else (ii >= jj)
        edge = 0 if reverse else c_len - 1
        start = g * g_len if isinstance(g, int) else pl.multiple_of(g * g_len, g_len)
        b_all = _chunk_scan(la_ref[pl.ds(start, g_len), kcols], reverse)
        s_t = st[srows, :]
        for c in (range(GLA_GROUP - 1, -1, -1) if reverse else range(GLA_GROUP)):
            sl = pl.ds(start + c * c_len, c_len)
            b = b_all[c * c_len:(c + 1) * c_len]
            btot = b[edge:edge + 1, :]
            qc = qs[sl, kcols]
            kc = ks[sl, kcols]
            vc = v_ref[sl, vcols]
            qe = (qc * jnp.exp(b)).astype(BF16)
            ke = (kc * jnp.exp(-b)).astype(BF16)
            kw = (kc * jnp.exp(btot - b)).astype(BF16)
            att = jnp.where(mask, _dot_nt(qe, ke), 0.0)
            out[sl, vcols] = _dot(att.astype(BF16), vc.astype(BF16)) + _dot_nt(qe, s_t.astype(BF16))
            s_t = jnp.exp(btot) * s_t + _dot(vc.T.astype(BF16), kw)
        st[srows, :] = s_t

    for direction, st in enumerate((st_f, st_b)):
        st[...] = s0_ref[direction].T if latent else jnp.zeros_like(st)
    if n_groups == 1:
        for hh in range(heads):
            half_group(0, False, hh)
            half_group(0, True, hh)
    else:
        def body(i, carry):
            for hh in range(heads):
                half_group(i, False, hh)
                half_group(n_groups - 1 - i, True, hh)
            return carry
        lax.fori_loop(0, n_groups, body, 0)

    for hh in range(heads):
        kcols, vcols = slice(hh * dk, (hh + 1) * dk), slice(hh * dv, (hh + 1) * dv)
        if not latent:
            sfin_ref[0, hh] = st_f[vcols, :].T
            sfin_ref[1, hh] = st_b[vcols, :].T
        o = (o_f[:, vcols] + o_b[:, vcols]) - jnp.sum(qs[:, kcols] * ks[:, kcols], -1, keepdims=True) * v_ref[:, vcols]
        o = o * lax.rsqrt(jnp.mean(o * o, -1, keepdims=True) + LN_EPS) * ng_ref[:, vcols]
        o_ref[:, vcols] = (o * _silu(z_ref[:, vcols])).astype(BF16)


def _gla(p, p_lr, wg_f, wg_b, b_gate, norm_g, bsz, seq_len, layer, rope=None, state=None, state_prev=()):
    latent = state is not None
    assert seq_len % (GLA_GROUP * GLA_CHUNK) == 0
    dk, dv = GLA_DK, GLA_DV
    hps = 1 if latent else GLA_HEADS
    kw, vw = hps * dk, hps * dv
    in_specs = [pl.BlockSpec((seq_len, kw), lambda b, h: (b, COL_BQ // kw + h)),
                pl.BlockSpec((seq_len, kw), lambda b, h: (b, COL_BK // kw + h)),
                pl.BlockSpec((seq_len, vw), lambda b, h: (b, COL_BV // vw + h)),
                pl.BlockSpec((seq_len, vw), lambda b, h: (b, COL_BZ // vw + h)),
                pl.BlockSpec((seq_len, LANES), lambda b, h: (b, 0)),
                pl.BlockSpec((LANES, kw), lambda b, h: (0, h)),
                pl.BlockSpec((LANES, kw), lambda b, h: (0, h)),
                pl.BlockSpec((2, kw), lambda b, h: (0, h)),
                pl.BlockSpec((1, vw), lambda b, h: (0, h))]
    args = [p, p, p, p, p_lr, wg_f, wg_b, b_gate, norm_g]
    o_spec = pl.BlockSpec((seq_len, vw), lambda b, h: (b, h))
    o_shape = jax.ShapeDtypeStruct((bsz * seq_len, GLA_VW), BF16)
    if latent:
        cos, sin = rope
        in_specs += [pl.BlockSpec((seq_len, dk), lambda b, h: (0, 0)),
                     pl.BlockSpec((seq_len, dk), lambda b, h: (0, 0)),
                     pl.BlockSpec((None, None, 2, None, dk, dv), lambda b, h: (b, layer, 0, h, 0, 0))]
        args += [cos, sin, state]
        out_specs, out_shape, aliases = o_spec, o_shape, {}
    else:
        aliases = {len(args) + i: 1 + i for i in range(len(state_prev))}
        in_specs += [pl.BlockSpec(memory_space=pl.ANY)] * len(state_prev)
        args += list(state_prev)
        out_specs = [o_spec, pl.BlockSpec((None, None, 2, hps, dk, dv), lambda b, h: (b, layer, 0, h, 0, 0))]
        out_shape = [o_shape, jax.ShapeDtypeStruct((bsz, DEPTH, 2, GLA_HEADS, dk, dv), F32)]
    return pl.pallas_call(
        functools.partial(_gla_kernel, latent=latent, n_groups=seq_len // (GLA_GROUP * GLA_CHUNK), heads=hps),
        grid=(bsz, GLA_HEADS // hps),
        in_specs=in_specs,
        out_specs=out_specs,
        out_shape=out_shape,
        input_output_aliases=aliases,
        scratch_shapes=[pltpu.VMEM((seq_len, kw), F32), pltpu.VMEM((seq_len, kw), F32),
                        pltpu.VMEM((seq_len, kw), F32), pltpu.VMEM((seq_len, kw), F32),
                        pltpu.VMEM((seq_len, vw), F32), pltpu.VMEM((seq_len, vw), F32),
                        pltpu.VMEM((vw, dk), F32), pltpu.VMEM((vw, dk), F32)],
        compiler_params=_cparams("parallel", "parallel"),
        name="gla_latent" if latent else "gla_context",
    )(*args)


def _conv_merge_kernel(u_ref, g_ref, up_ref, gp_ref, un_ref, gn_ref, z_ref, w_ref, b_ref, ng_ref, nb_ref,
                       oa_ref, ob_ref, ga_ref, gb_ref, gc_ref, wa_ref, wb_ref, wc_ref, o_ref, pad, cv,
                       *, tiles_per_seq, tt):
    m = jax.nn.sigmoid(ga_ref[...]) * _dot(oa_ref[...], wa_ref[...])
    m = m + jax.nn.sigmoid(gb_ref[...]) * _dot(ob_ref[...], wb_ref[...])
    t = pl.program_id(0) % tiles_per_seq
    halo = CONV_HALO
    glu = lambda u, g: u * jax.nn.sigmoid(g)
    pad[halo:halo + tt, :] = glu(u_ref[...], g_ref[...])
    pad[0:halo, :] = jnp.where(t > 0, glu(up_ref[...], gp_ref[...]), 0.0)
    pad[halo + tt:halo + tt + halo, :] = jnp.where(t < tiles_per_seq - 1, glu(un_ref[...], gn_ref[...]), 0.0)
    first = halo - CONV_TAPS // 2
    rb = 64
    reach = -(-(first + CONV_TAPS - 1) // SUBLANES) * SUBLANES
    for c0 in range(0, CONV_WIDTH, LANES):
        lanes = slice(c0, c0 + LANES)
        w_strip = w_ref[:, lanes]
        for r0 in range(0, tt, rb):
            base = pad[r0:r0 + rb + reach, lanes]
            acc = jnp.broadcast_to(b_ref[:, lanes], (rb, LANES))
            for phase in range(SUBLANES):
                rows = pltpu.roll(base, rb + reach - phase, 0) if phase else base
                for a in range(0, reach, SUBLANES):
                    j = a + phase - first
                    if 0 <= j < CONV_TAPS:
                        acc = acc + rows[a:a + rb] * w_strip[j:j + 1, :]
            cv[r0:r0 + rb, lanes] = acc
    acc = cv[...]
    mu = jnp.mean(acc, -1, keepdims=True)
    xc = acc - mu
    var = jnp.mean(xc * xc, -1, keepdims=True)
    y = xc * lax.rsqrt(var + LN_EPS) * ng_ref[...] + nb_ref[...]
    oc = (_silu(y) * _silu(z_ref[...])).astype(BF16)
    m = m + jax.nn.sigmoid(gc_ref[...]) * _dot(oc, wc_ref[...])
    o_ref[...] = m.astype(BF16)


def _conv_merge(p, og_a, og_b, conv_w, conv_b, norm_g, norm_b, wa, wb, wc, seq_len):
    t_tok = p.shape[0]
    tt = 256
    tiles_per_seq = seq_len // tt
    w = CONV_WIDTH
    d = D_MODEL
    hb = tt // CONV_HALO
    n_halo = t_tok // CONV_HALO
    main = lambda col: pl.BlockSpec((tt, w), lambda i: (i, col // w))
    prev = lambda col: pl.BlockSpec((CONV_HALO, w), lambda i: (jnp.maximum(i * hb - 1, 0), col // w))
    nxt = lambda col: pl.BlockSpec((CONV_HALO, w), lambda i: (jnp.minimum((i + 1) * hb, n_halo - 1), col // w))
    vec = pl.BlockSpec((1, w), lambda i: (0, 0))
    br = pl.BlockSpec((tt, w), lambda i: (i, 0))
    gate = lambda col: pl.BlockSpec((tt, d), lambda i: (i, col // d))
    wspec = pl.BlockSpec((w, d), lambda i: (0, 0), pipeline_mode=pl.Buffered(1))
    return pl.pallas_call(
        functools.partial(_conv_merge_kernel, tiles_per_seq=tiles_per_seq, tt=tt),
        grid=(t_tok // tt,),
        in_specs=[main(COL_CU), main(COL_CG), prev(COL_CU), prev(COL_CG), nxt(COL_CU), nxt(COL_CG), main(COL_CZ),
                  pl.BlockSpec((CONV_TAPS, w), lambda i: (0, 0)), vec, vec, vec,
                  br, br, gate(COL_GA), gate(COL_GB), gate(COL_GC), wspec, wspec, wspec],
        out_specs=pl.BlockSpec((tt, d), lambda i: (i, 0)),
        out_shape=jax.ShapeDtypeStruct((t_tok, d), BF16),
        scratch_shapes=[pltpu.VMEM((tt + 2 * CONV_HALO, w), F32), pltpu.VMEM((tt, w), F32)],
        compiler_params=_cparams("parallel"),
        name="conv_merge",
    )(p, p, p, p, p, p, p, conv_w, conv_b, norm_g, norm_b, og_a, og_b, p, p, p, wa, wb, wc)


OUT_SUB = 256


def _out_kernel(m_ref, w_ref, x_ref, mod_ref, g_ref, b_ref, o_ref):
    gate = mod_ref[0, :, 2 * D_MODEL:3 * D_MODEL]
    for r0 in range(0, m_ref.shape[0], OUT_SUB):
        rows = slice(r0, r0 + OUT_SUB)
        y = ALPHA * x_ref[rows, :] + gate * _dot(m_ref[rows, :], w_ref[...])
        mu = jnp.mean(y, -1, keepdims=True)
        yc = y - mu
        var = jnp.mean(yc * yc, -1, keepdims=True)
        o_ref[rows, :] = yc * lax.rsqrt(var + LN_EPS) * g_ref[...] + b_ref[...]


def _out_projection(merged, w_out, x2d, mod, ln_g, ln_b, seq_len):
    t_tok = x2d.shape[0]
    tm = 2 * OUT_SUB
    d = D_MODEL
    per = seq_len // tm if mod.shape[0] > 1 else t_tok // tm
    row = pl.BlockSpec((tm, d), lambda i: (i, 0))
    vec = pl.BlockSpec((1, d), lambda i: (0, 0))
    return pl.pallas_call(
        _out_kernel,
        grid=(t_tok // tm,),
        in_specs=[row, pl.BlockSpec((d, d), lambda i: (0, 0), pipeline_mode=pl.Buffered(1)), row,
                  pl.BlockSpec((1, 1, 3 * d), lambda i: (i // per, 0, 0)), vec, vec],
        out_specs=row,
        out_shape=jax.ShapeDtypeStruct((t_tok, d), F32),
        compiler_params=_cparams("parallel"),
        name="out_projection",
    )(merged, w_out, x2d, mod, ln_g, ln_b)


def kernel(x_prompt, x_sample, cache_k, cache_v, state_gla, c, c_ctx, w_mod, b_mod, w_in, rpb, gla_w_gate,
           gla_b_gate, gla_norm_g, conv_w, conv_b, conv_norm_g, conv_norm_b, w_proj_a, w_proj_b, w_proj_c,
           w_out, ln_g, ln_b):
    bsz, seq_len, d = x_prompt.shape
    dbsz, dseq, _ = x_sample.shape
    past = cache_k.shape[2]

    cvec = jnp.zeros((8, d), F32).at[0].set(c_ctx).at[1:1 + dbsz].set(c)
    mod_all = _modulation(cvec, w_mod, b_mod)

    w_in_t = jnp.swapaxes(w_in, 1, 2)
    w_t = _cast_rows_bf16(w_in_t)
    w_lr_t = jnp.pad(w_in_t[:, LR_START:LR_START + 2 * GLA_LOWRANK],
                     ((0, 0), (0, LANES - 2 * GLA_LOWRANK), (0, 0))).astype(BF16)
    wg = gla_w_gate.astype(BF16)
    wg_f = jnp.pad(wg[:, 0], ((0, 0), (0, LANES - GLA_LOWRANK), (0, 0)))
    wg_b = jnp.pad(wg[:, 1], ((0, 0), (GLA_LOWRANK, LANES - 2 * GLA_LOWRANK), (0, 0)))
    wpa, wpb, wpc, wo = (w.astype(BF16) for w in (w_proj_a, w_proj_b, w_proj_c, w_out))
    ck = cache_k.reshape(dbsz, DEPTH, past, NA_WIDTH)
    cv = cache_v.reshape(dbsz, DEPTH, past, NA_WIDTH)
    rope = _rope_tables(dseq)

    h_ctx = x_prompt.reshape(bsz * seq_len, d)
    h_lat = x_sample.reshape(dbsz * dseq, d)
    kv_new, s_new = (), ()
    for l in range(DEPTH):
        mod_ctx = mod_all[l, 0:1][:, None, :]
        mod_lat = mod_all[l, 1:1 + dbsz][:, None, :]
        row = lambda a: a[l][None, :]
        gla_w = (wg_f[l], wg_b[l], gla_b_gate[l], row(gla_norm_g))
        conv_p = (conv_w[l], row(conv_b), row(conv_norm_g), row(conv_norm_b))

        def tail(x2d, p, og_a, og_b, mod, seq):
            merged = _conv_merge(p, og_a, og_b, *conv_p, wpa[l], wpb[l], wpc[l], seq)
            return _out_projection(merged, wo[l], x2d, mod, row(ln_g), row(ln_b), seq)

        p, p_lr = _in_projection(h_ctx, mod_ctx, w_t, w_lr_t, l, seq_len)
        og_a, *kv_new = _context_attention(p, bsz, seq_len, l, kv_new)
        og_b, *s_new = _gla(p, p_lr, *gla_w, bsz, seq_len, l, state_prev=s_new)
        h_ctx = tail(h_ctx, p, og_a, og_b, mod_ctx, seq_len)

        p, p_lr = _in_projection(h_lat, mod_lat, w_t, w_lr_t, l, dseq)
        og_a = _neighbourhood_attention(p, ck, cv, rpb[l], l, dbsz, dseq)
        og_b = _gla(p, p_lr, *gla_w, dbsz, dseq, l, rope=rope, state=state_gla)
        h_lat = tail(h_lat, p, og_a, og_b, mod_lat, dseq)

    new_k, new_v = (a.reshape(bsz, DEPTH, seq_len, NA_HEADS, NA_HEAD_DIM) for a in kv_new)
    return (h_ctx.reshape(bsz, seq_len, d), h_lat.reshape(dbsz, dseq, d), new_k, new_v, s_new[0])
```

```python
import functools

import numpy as np
import jax
import jax.numpy as jnp
from jax import lax
from jax.experimental import pallas as pl
from jax.experimental.pallas import tpu as pltpu

D_MODEL = 2048
DEPTH = 2
GRID_W = 64
NA_HEADS = 8
NA_HEAD_DIM = 128
NA_WIDTH = NA_HEADS * NA_HEAD_DIM
NA_WIN_H = 8
NA_WIN_W = 16
GLA_HEADS = 4
GLA_DK = 128
GLA_DV = 256
GLA_KW = GLA_HEADS * GLA_DK
GLA_VW = GLA_HEADS * GLA_DV
GLA_LOWRANK = 16
GLA_TAU = 16.0
GLA_CHUNK = 64
ROPE_BASE = 10000.0
CONV_WIDTH = 1024
CONV_TAPS = 31
ALPHA = (2 * DEPTH) ** 0.25
LN_EPS = 1e-5
NEG_INF = -1e30

LANES = 128
SUBLANES = 8
BF16_ROWS = 2 * SUBLANES
CONV_HALO = 16
VMEM_LIMIT = 56 * 1024 * 1024

COL_AQ, COL_AK, COL_AV, COL_AZ = 0, 1024, 2048, 3072
COL_BQ, COL_BK, COL_BV, COL_BZ = 4096, 4608, 5120, 6144
COL_CU, COL_CG, COL_CZ = 7168, 8192, 9216
COL_GA, COL_GB, COL_GC = 10240, 12288, 14336
N_MAIN = 16384
LR_START = 6144

BF16 = jnp.bfloat16
F32 = jnp.float32


def _cparams(*sem):
    return pltpu.CompilerParams(dimension_semantics=sem, vmem_limit_bytes=VMEM_LIMIT)


def _silu(x):
    return x * jax.nn.sigmoid(x)


def _dot(a, b):
    return jnp.dot(a, b, preferred_element_type=F32)


def _dot_nt(a, b):
    return lax.dot_general(a, b, (((1,), (1,)), ((), ())), preferred_element_type=F32)


def _mod_kernel(cv_ref, w_ref, b_ref, o_ref):
    s = _silu(cv_ref[...]).astype(BF16)
    o_ref[0] = _dot(s, w_ref[0].astype(BF16)) + b_ref[0]


def _modulation(cvec, w_mod, b_mod):
    tn = 1024
    n3 = 3 * D_MODEL
    return pl.pallas_call(
        _mod_kernel,
        grid=(DEPTH, n3 // tn),
        in_specs=[pl.BlockSpec((8, D_MODEL), lambda l, n: (0, 0)),
                  pl.BlockSpec((1, D_MODEL, tn), lambda l, n: (l, 0, n)),
                  pl.BlockSpec((1, 1, tn), lambda l, n: (l, 0, n))],
        out_specs=pl.BlockSpec((1, 8, tn), lambda l, n: (l, 0, n)),
        out_shape=jax.ShapeDtypeStruct((DEPTH, 8, n3), F32),
        compiler_params=_cparams("parallel", "parallel"),
        name="modulation",
    )(cvec, w_mod, b_mod.reshape(DEPTH, 1, n3))


def _cast_kernel(a_ref, o_ref):
    o_ref[...] = a_ref[...].astype(BF16)


def _cast_rows_bf16(w_t):
    depth, rows, d = w_t.shape
    tr = max(t for t in range(BF16_ROWS, 1024 + 1, BF16_ROWS) if rows % t == 0)
    spec = pl.BlockSpec((1, tr, d), lambda l, r: (l, r, 0))
    return pl.pallas_call(
        _cast_kernel,
        grid=(depth, rows // tr),
        in_specs=[spec],
        out_specs=spec,
        out_shape=jax.ShapeDtypeStruct(w_t.shape, BF16),
        compiler_params=_cparams("parallel", "parallel"),
        name="cast_w_in",
    )(w_t)


def _inproj_kernel(x_ref, mod_ref, w_ref, wlr_ref, o_ref, olr_ref, h_scr):
    strip = 256

    @pl.when(pl.program_id(1) == 0)
    def _():
        shift = mod_ref[0, :, 0:D_MODEL]
        scale = mod_ref[0, :, D_MODEL:2 * D_MODEL]

        def body(i, carry):
            sl = pl.ds(pl.multiple_of(i * strip, strip), strip)
            x = x_ref[sl, :]
            mu = jnp.mean(x, -1, keepdims=True)
            xc = x - mu
            var = jnp.mean(xc * xc, -1, keepdims=True)
            y = xc * lax.rsqrt(var + LN_EPS)
            hb = (y * (1.0 + scale) + shift).astype(BF16)
            h_scr[sl, :] = hb
            olr_ref[sl, :] = _dot_nt(hb, wlr_ref[...])
            return carry

        lax.fori_loop(0, x_ref.shape[0] // strip, body, 0)

    o_ref[...] = _dot_nt(h_scr[...], w_ref[0])


def _in_projection(x2d, mod, w_t, w_lr_t, layer, seq_len):
    t_tok = x2d.shape[0]
    tm, tn = 1024, 1024
    assert LR_START % tn == 0
    per = seq_len // tm if mod.shape[0] > 1 else t_tok // tm
    w_row = lambda n: pl.multiple_of(n * tn + jnp.where(n >= LR_START // tn, 2 * GLA_LOWRANK, 0), 2 * GLA_LOWRANK)
    return pl.pallas_call(
        _inproj_kernel,
        grid=(t_tok // tm, N_MAIN // tn),
        in_specs=[pl.BlockSpec((tm, D_MODEL), lambda m, n: (m, 0)),
                  pl.BlockSpec((1, 1, 3 * D_MODEL), lambda m, n: (m // per, 0, 0)),
                  pl.BlockSpec((pl.Element(1), pl.Element(tn), pl.Element(D_MODEL)),
                               lambda m, n: (layer, w_row(n), 0)),
                  pl.BlockSpec((None, LANES, D_MODEL), lambda m, n: (layer, 0, 0))],
        out_specs=[pl.BlockSpec((tm, tn), lambda m, n: (m, n)),
                   pl.BlockSpec((tm, LANES), lambda m, n: (m, 0))],
        out_shape=[jax.ShapeDtypeStruct((t_tok, N_MAIN), F32),
                   jax.ShapeDtypeStruct((t_tok, LANES), F32)],
        scratch_shapes=[pltpu.VMEM((tm, D_MODEL), BF16)],
        compiler_params=_cparams("parallel", "arbitrary"),
        name="in_projection",
    )(x2d, mod, w_t, w_lr_t)


def _ctx_attn_kernel(q_ref, k_ref, v_ref, z_ref, o_ref):
    scale = NA_HEAD_DIM ** -0.5
    for h in range(NA_HEADS):
        cs = slice(h * NA_HEAD_DIM, (h + 1) * NA_HEAD_DIM)
        q = q_ref[:, cs].astype(BF16)
        k = k_ref[:, cs].astype(BF16)
        v = v_ref[:, cs].astype(BF16)
        s = _dot_nt(q, k) * scale
        e = jnp.exp(s - jnp.max(s, -1, keepdims=True))
        p = e / jnp.sum(e, -1, keepdims=True)
        o = _dot(p.astype(BF16), v)
        o_ref[:, cs] = (o * _silu(z_ref[:, cs])).astype(BF16)


def _context_attention(p_ctx, bsz, seq_len):
    blk = lambda j: pl.BlockSpec((seq_len, NA_WIDTH), lambda b: (b, j))
    return pl.pallas_call(
        _ctx_attn_kernel,
        grid=(bsz,),
        in_specs=[blk(COL_AQ // NA_WIDTH), blk(COL_AK // NA_WIDTH), blk(COL_AV // NA_WIDTH), blk(COL_AZ // NA_WIDTH)],
        out_specs=pl.BlockSpec((seq_len, NA_WIDTH), lambda b: (b, 0)),
        out_shape=jax.ShapeDtypeStruct((bsz * seq_len, NA_WIDTH), BF16),
        compiler_params=_cparams("parallel"),
        name="context_attention",
    )(p_ctx, p_ctx, p_ctx, p_ctx)


NA_QROWS = 4
NA_KROWS = NA_QROWS + NA_WIN_H


NA_NDR = 2 * NA_WIN_H


def _na_plan(rows):
    kh = min(NA_WIN_H, rows)
    masked = NA_NDR - 1
    groups = []
    for r0 in range(0, rows, NA_QROWS):
        ks = int(np.clip(r0 - kh // 2, 0, rows - NA_KROWS))
        tiles = []
        for rq in range(NA_QROWS):
            r = r0 + rq
            rs = int(np.clip(r - kh // 2, 0, rows - kh))
            slot = [ks + i - r + NA_WIN_H - 1 if rs <= ks + i < rs + kh else masked for i in range(NA_KROWS)]
            row_tiles = []
            for i in range(0, NA_KROWS, 2):
                a, b = slot[i], slot[i + 1]
                if a != masked and b != masked:
                    row_tiles.append(b)
                elif b != masked:
                    row_tiles.append(NA_NDR + b)
                else:
                    row_tiles.append(2 * NA_NDR + a)
            tiles.append(tuple(row_tiles))
        groups.append((r0, ks, tuple(tiles)))
    return tuple(groups)


def _na_bias_tiles(rpb_l):
    nh = rpb_l.shape[0]
    n_dr = 2 * NA_WIN_H - 1
    c = np.arange(GRID_W)
    cs = np.clip(c - NA_WIN_W // 2, 0, GRID_W - NA_WIN_W)
    valid = (c[None, :] >= cs[:, None]) & (c[None, :] < cs[:, None] + NA_WIN_W)
    span = 2 * GRID_W
    left = (GRID_W - 1) - (NA_WIN_W - 1)
    wv = jnp.pad(rpb_l.astype(F32), ((0, 0), (0, 0), (left, span - left - (2 * NA_WIN_W - 1))))
    skew = jnp.tile(wv, (1, 1, GRID_W))[..., :GRID_W * (span - 1)].reshape(nh, n_dr, GRID_W, span - 1)
    toe = jnp.where(valid, skew[..., GRID_W - 1:], NEG_INF)
    neg1 = jnp.full((nh, 1, GRID_W, GRID_W), NEG_INF, F32)
    negs = jnp.full((nh, NA_NDR, GRID_W, GRID_W), NEG_INF, F32)
    cur = jnp.concatenate([toe, neg1], axis=1)
    prev = jnp.concatenate([neg1, toe], axis=1)
    return jnp.concatenate([jnp.concatenate([prev, cur], -1), jnp.concatenate([negs, cur], -1),
                            jnp.concatenate([cur, negs], -1)], axis=1)


def _na_kernel(q_ref, k_ref, v_ref, z_ref, kc_ref, vc_ref, bias_ref, o_ref, kb_scr, vb_scr, *, groups):
    scale = NA_HEAD_DIM ** -0.5
    kb_scr[...] = k_ref[...].astype(BF16)
    vb_scr[...] = v_ref[...].astype(BF16)
    kctx = kc_ref[...].astype(BF16)
    vctx = vc_ref[...].astype(BF16)
    nq, nk = NA_QROWS * GRID_W, NA_KROWS * GRID_W
    for r0, ks, tiles in groups:
        qsl = slice(r0 * GRID_W, r0 * GRID_W + nq)
        ksl = slice(ks * GRID_W, ks * GRID_W + nk)
        q = q_ref[qsl, :].astype(BF16)
        bias = jnp.concatenate([jnp.concatenate([bias_ref[t] for t in row], axis=1) for row in tiles], axis=0)
        s_win = _dot_nt(q, kb_scr[ksl, :]) * scale + bias
        s_ctx = _dot_nt(q, kctx) * scale
        m = jnp.maximum(jnp.max(s_win, -1, keepdims=True), jnp.max(s_ctx, -1, keepdims=True))
        e_win = jnp.exp(s_win - m)
        e_ctx = jnp.exp(s_ctx - m)
        den = jnp.sum(e_win, -1, keepdims=True) + jnp.sum(e_ctx, -1, keepdims=True)
        o = _dot((e_win / den).astype(BF16), vb_scr[ksl, :]) + _dot((e_ctx / den).astype(BF16), vctx)
        o_ref[qsl, :] = (o * _silu(z_ref[qsl, :])).astype(BF16)


def _neighbourhood_attention(p_lat, cache_k, cache_v, rpb_l, layer, bsz, n_tok):
    rows = n_tok // GRID_W
    assert rows % NA_QROWS == 0 and rows >= NA_KROWS and NA_KROWS % 2 == 0
    groups = _na_plan(rows)
    bias_tab = _na_bias_tiles(rpb_l)
    past = cache_k.shape[2]
    hd = NA_HEAD_DIM
    blk = lambda col: pl.BlockSpec((n_tok, hd), lambda b, h: (b, col // hd + h))
    cblk = pl.BlockSpec((None, None, past, hd), lambda b, h: (b, layer, 0, h))
    return pl.pallas_call(
        functools.partial(_na_kernel, groups=groups),
        grid=(bsz, NA_HEADS),
        in_specs=[blk(COL_AQ), blk(COL_AK), blk(COL_AV), blk(COL_AZ), cblk, cblk,
                  pl.BlockSpec((None,) + bias_tab.shape[1:], lambda b, h: (h, 0, 0, 0))],
        out_specs=pl.BlockSpec((n_tok, hd), lambda b, h: (b, h)),
        out_shape=jax.ShapeDtypeStruct((bsz * n_tok, NA_WIDTH), BF16),
        scratch_shapes=[pltpu.VMEM((n_tok, hd), BF16), pltpu.VMEM((n_tok, hd), BF16)],
        compiler_params=_cparams("parallel", "parallel"),
        name="neighbourhood_attention",
    )(p_lat, p_lat, p_lat, p_lat, cache_k, cache_v, bias_tab)


def _rope_tables(seq_len):
    t = jnp.arange(seq_len)
    half = GLA_DK // 2
    inv = ROPE_BASE ** (-jnp.arange(0, half, 2, dtype=F32) / half)

    def tab(pos):
        ang = pos.astype(F32)[:, None] * inv[None, :]
        return jnp.cos(ang), jnp.sin(ang)

    cr, sr = tab(t // GRID_W)
    cc, sc = tab(t % GRID_W)
    cos = jnp.concatenate([cr, cr, cc, cc], -1)
    sin = jnp.concatenate([-sr, sr, -sc, sc], -1)
    return cos, sin


def _rope(x, cos, sin):
    quarter = GLA_DK // 4
    lane = lax.broadcasted_iota(jnp.int32, x.shape, 1)
    partner = jnp.where((lane % (2 * quarter)) < quarter,
                        pltpu.roll(x, GLA_DK - quarter, 1), pltpu.roll(x, quarter, 1))
    return x * cos + partner * sin


def _log_sigmoid(x):
    return jnp.minimum(x, 0.0) - jnp.log1p(jnp.exp(-jnp.abs(x)))


def _chunk_scan(x, reverse):
    n = x.shape[0]
    pos = lax.broadcasted_iota(jnp.int32, x.shape, 0) % GLA_CHUNK
    step = 1
    while step < GLA_CHUNK:
        if reverse:
            x = x + jnp.where(pos < GLA_CHUNK - step, pltpu.roll(x, n - step, 0), 0.0)
        else:
            x = x + jnp.where(pos >= step, pltpu.roll(x, step, 0), 0.0)
        step *= 2
    return x


GLA_GROUP = 4


def _gla_kernel(*refs, latent, n_groups, heads, layer):
    (q_ref, k_ref, v_ref, z_ref, lr_ref, wgf_ref, wgb_ref, bg_ref, ng_ref), rest = refs[:9], refs[9:]
    if latent:
        cos_ref, sin_ref, s0_ref, o_ref, qs, ks, laf, lab, o_f, o_b, st_f, st_b = rest
    else:
        ak_ref, av_ref = rest[:2]
        o_ref, sfin_ref, ko_ref, vo_ref, qs, ks, laf, lab, o_f, o_b, st_f, st_b = rest[-12:]
        if len(rest) == 14:
            for l in range(DEPTH):
                if l != layer:
                    sfin_ref[l] = jnp.zeros(sfin_ref.shape[1:], F32)
                    ko_ref[l] = jnp.zeros(ko_ref.shape[1:], F32)
                    vo_ref[l] = jnp.zeros(vo_ref.shape[1:], F32)
            sfin_ref, ko_ref, vo_ref = sfin_ref.at[layer], ko_ref.at[layer], vo_ref.at[layer]
        ko_ref[...] = ak_ref[...]
        vo_ref[...] = av_ref[...]
    c_len = GLA_CHUNK
    g_len = GLA_GROUP * c_len
    dk, dv = GLA_DK, GLA_DV

    q = q_ref[...] * (GLA_DK ** -0.5)
    k = k_ref[...]
    if latent:
        q = _rope(q, cos_ref[...], sin_ref[...])
        k = _rope(k, cos_ref[...], sin_ref[...])
    qs[...] = q
    ks[...] = k
    lrb = lr_ref[...].astype(BF16)
    laf[...] = _log_sigmoid(_dot(lrb, wgf_ref[...]) + bg_ref[0:1, :]) * (1.0 / GLA_TAU)
    lab[...] = _log_sigmoid(_dot(lrb, wgb_ref[...]) + bg_ref[1:2, :]) * (1.0 / GLA_TAU)

    ii = lax.broadcasted_iota(jnp.int32, (c_len, c_len), 0)
    jj = lax.broadcasted_iota(jnp.int32, (c_len, c_len), 1)

    def half_group(g, reverse, hh):
        la_ref, out, st = (lab, o_b, st_b) if reverse else (laf, o_f, st_f)
        kcols, vcols, srows = slice(hh * dk, (hh + 1) * dk), slice(hh * dv, (hh + 1) * dv), slice(hh * dv, (hh + 1) * dv)
        mask = (ii <= jj) if reverse else (ii >= jj)
        edge = 0 if reverse else c_len - 1
        start = g * g_len if isinstance(g, int) else pl.multiple_of(g * g_len, g_len)
        b_all = _chunk_scan(la_ref[pl.ds(start, g_len), kcols], reverse)
        s_t = st[srows, :]
        for c in (range(GLA_GROUP - 1, -1, -1) if reverse else range(GLA_GROUP)):
            sl = pl.ds(start + c * c_len, c_len)
            b = b_all[c * c_len:(c + 1) * c_len]
            btot = b[edge:edge + 1, :]
            qc = qs[sl, kcols]
            kc = ks[sl, kcols]
            vc = v_ref[sl, vcols]
            qe = (qc * jnp.exp(b)).astype(BF16)
            ke = (kc * jnp.exp(-b)).astype(BF16)
            kw = (kc * jnp.exp(btot - b)).astype(BF16)
            att = jnp.where(mask, _dot_nt(qe, ke), 0.0)
            out[sl, vcols] = _dot(att.astype(BF16), vc.astype(BF16)) + _dot_nt(qe, s_t.astype(BF16))
            s_t = jnp.exp(btot) * s_t + _dot(vc.T.astype(BF16), kw)
        st[srows, :] = s_t

    for direction, st in enumerate((st_f, st_b)):
        st[...] = s0_ref[direction].T if latent else jnp.zeros_like(st)
    if n_groups == 1:
        for hh in range(heads):
            half_group(0, False, hh)
            half_group(0, True, hh)
    else:
        def body(i, carry):
            for hh in range(heads):
                half_group(i, False, hh)
                half_group(n_groups - 1 - i, True, hh)
            return carry
        lax.fori_loop(0, n_groups, body, 0)

    for hh in range(heads):
        kcols, vcols = slice(hh * dk, (hh + 1) * dk), slice(hh * dv, (hh + 1) * dv)
        if not latent:
            sfin_ref[0, hh] = st_f[vcols, :].T
            sfin_ref[1, hh] = st_b[vcols, :].T
        o = (o_f[:, vcols] + o_b[:, vcols]) - jnp.sum(qs[:, kcols] * ks[:, kcols], -1, keepdims=True) * v_ref[:, vcols]
        o = o * lax.rsqrt(jnp.mean(o * o, -1, keepdims=True) + LN_EPS) * ng_ref[:, vcols]
        o_ref[:, vcols] = (o * _silu(z_ref[:, vcols])).astype(BF16)


def _gla(p, p_lr, wg_f, wg_b, b_gate, norm_g, bsz, seq_len, layer, rope=None, state=None, state_prev=()):
    latent = state is not None
    assert seq_len % (GLA_GROUP * GLA_CHUNK) == 0
    dk, dv = GLA_DK, GLA_DV
    hps = 1 if latent else GLA_HEADS
    kw, vw = hps * dk, hps * dv
    in_specs = [pl.BlockSpec((seq_len, kw), lambda b, h: (b, COL_BQ // kw + h)),
                pl.BlockSpec((seq_len, kw), lambda b, h: (b, COL_BK // kw + h)),
                pl.BlockSpec((seq_len, vw), lambda b, h: (b, COL_BV // vw + h)),
                pl.BlockSpec((seq_len, vw), lambda b, h: (b, COL_BZ // vw + h)),
                pl.BlockSpec((seq_len, LANES), lambda b, h: (b, 0)),
                pl.BlockSpec((LANES, kw), lambda b, h: (0, h)),
                pl.BlockSpec((LANES, kw), lambda b, h: (0, h)),
                pl.BlockSpec((2, kw), lambda b, h: (0, h)),
                pl.BlockSpec((1, vw), lambda b, h: (0, h))]
    args = [p, p, p, p, p_lr, wg_f, wg_b, b_gate, norm_g]
    o_spec = pl.BlockSpec((seq_len, vw), lambda b, h: (b, h))
    o_shape = jax.ShapeDtypeStruct((bsz * seq_len, GLA_VW), BF16)
    if latent:
        cos, sin = rope
        in_specs += [pl.BlockSpec((seq_len, dk), lambda b, h: (0, 0)),
                     pl.BlockSpec((seq_len, dk), lambda b, h: (0, 0)),
                     pl.BlockSpec((None, None, 2, None, dk, dv), lambda b, h: (b, layer, 0, h, 0, 0))]
        args += [cos, sin, state]
        out_specs, out_shape, aliases = o_spec, o_shape, {}
    else:
        assert hps == GLA_HEADS
        in_specs += [pl.BlockSpec((seq_len, NA_WIDTH), lambda b, h: (b, COL_AK // NA_WIDTH)),
                     pl.BlockSpec((seq_len, NA_WIDTH), lambda b, h: (b, COL_AV // NA_WIDTH))]
        args += [p, p]
        aliases = {len(args) + i: 1 + i for i in range(len(state_prev))}
        in_specs += [pl.BlockSpec(memory_space=pl.ANY)] * len(state_prev)
        args += list(state_prev)
        if state_prev:
            s_spec = pl.BlockSpec((None, None, 2, hps, dk, dv), lambda b, h: (b, layer, 0, h, 0, 0))
            kv_spec = pl.BlockSpec((None, None, seq_len, NA_WIDTH), lambda b, h: (b, layer, 0, 0))
        else:
            s_spec = pl.BlockSpec((None, DEPTH, 2, hps, dk, dv), lambda b, h: (b, 0, 0, h, 0, 0))
            kv_spec = pl.BlockSpec((None, DEPTH, seq_len, NA_WIDTH), lambda b, h: (b, 0, 0, 0))
        kv_shape = jax.ShapeDtypeStruct((bsz, DEPTH, seq_len, NA_WIDTH), F32)
        out_specs = [o_spec, s_spec, kv_spec, kv_spec]
        out_shape = [o_shape, jax.ShapeDtypeStruct((bsz, DEPTH, 2, GLA_HEADS, dk, dv), F32), kv_shape, kv_shape]
    return pl.pallas_call(
        functools.partial(_gla_kernel, latent=latent, n_groups=seq_len // (GLA_GROUP * GLA_CHUNK), heads=hps,
                          layer=layer),
        grid=(bsz, GLA_HEADS // hps),
        in_specs=in_specs,
        out_specs=out_specs,
        out_shape=out_shape,
        input_output_aliases=aliases,
        scratch_shapes=[pltpu.VMEM((seq_len, kw), F32), pltpu.VMEM((seq_len, kw), F32),
                        pltpu.VMEM((seq_len, kw), F32), pltpu.VMEM((seq_len, kw), F32),
                        pltpu.VMEM((seq_len, vw), F32), pltpu.VMEM((seq_len, vw), F32),
                        pltpu.VMEM((vw, dk), F32), pltpu.VMEM((vw, dk), F32)],
        compiler_params=_cparams("parallel", "parallel"),
        name="gla_latent" if latent else "gla_context",
    )(*args)


def _conv_merge_kernel(u_ref, g_ref, up_ref, gp_ref, un_ref, gn_ref, z_ref, w_ref, b_ref, ng_ref, nb_ref,
                       oa_ref, ob_ref, ga_ref, gb_ref, gc_ref, wa_ref, wb_ref, wc_ref, o_ref, pad, cv,
                       *, tiles_per_seq, tt):
    m = jax.nn.sigmoid(ga_ref[...]) * _dot(oa_ref[...], wa_ref[...])
    m = m + jax.nn.sigmoid(gb_ref[...]) * _dot(ob_ref[...], wb_ref[...])
    t = pl.program_id(0) % tiles_per_seq
    halo = CONV_HALO
    glu = lambda u, g: u * jax.nn.sigmoid(g)
    pad[halo:halo + tt, :] = glu(u_ref[...], g_ref[...])
    pad[0:halo, :] = jnp.where(t > 0, glu(up_ref[...], gp_ref[...]), 0.0)
    pad[halo + tt:halo + tt + halo, :] = jnp.where(t < tiles_per_seq - 1, glu(un_ref[...], gn_ref[...]), 0.0)
    first = halo - CONV_TAPS // 2
    rb = 64
    reach = -(-(first + CONV_TAPS - 1) // SUBLANES) * SUBLANES
    for c0 in range(0, CONV_WIDTH, LANES):
        lanes = slice(c0, c0 + LANES)
        w_strip = w_ref[:, lanes]
        for r0 in range(0, tt, rb):
            base = pad[r0:r0 + rb + reach, lanes]
            acc = jnp.broadcast_to(b_ref[:, lanes], (rb, LANES))
            for phase in range(SUBLANES):
                rows = pltpu.roll(base, rb + reach - phase, 0) if phase else base
                for a in range(0, reach, SUBLANES):
                    j = a + phase - first
                    if 0 <= j < CONV_TAPS:
                        acc = acc + rows[a:a + rb] * w_strip[j:j + 1, :]
            cv[r0:r0 + rb, lanes] = acc
    acc = cv[...]
    mu = jnp.mean(acc, -1, keepdims=True)
    xc = acc - mu
    var = jnp.mean(xc * xc, -1, keepdims=True)
    y = xc * lax.rsqrt(var + LN_EPS) * ng_ref[...] + nb_ref[...]
    oc = (_silu(y) * _silu(z_ref[...])).astype(BF16)
    m = m + jax.nn.sigmoid(gc_ref[...]) * _dot(oc, wc_ref[...])
    o_ref[...] = m.astype(BF16)


def _conv_merge(p, og_a, og_b, conv_w, conv_b, norm_g, norm_b, wa, wb, wc, seq_len):
    t_tok = p.shape[0]
    tt = 256
    tiles_per_seq = seq_len // tt
    w = CONV_WIDTH
    d = D_MODEL
    hb = tt // CONV_HALO
    n_halo = t_tok // CONV_HALO
    main = lambda col: pl.BlockSpec((tt, w), lambda i: (i, col // w))
    prev = lambda col: pl.BlockSpec((CONV_HALO, w), lambda i: (jnp.maximum(i * hb - 1, 0), col // w))
    nxt = lambda col: pl.BlockSpec((CONV_HALO, w), lambda i: (jnp.minimum((i + 1) * hb, n_halo - 1), col // w))
    vec = pl.BlockSpec((1, w), lambda i: (0, 0))
    br = pl.BlockSpec((tt, w), lambda i: (i, 0))
    gate = lambda col: pl.BlockSpec((tt, d), lambda i: (i, col // d))
    wspec = pl.BlockSpec((w, d), lambda i: (0, 0), pipeline_mode=pl.Buffered(1))
    return pl.pallas_call(
        functools.partial(_conv_merge_kernel, tiles_per_seq=tiles_per_seq, tt=tt),
        grid=(t_tok // tt,),
        in_specs=[main(COL_CU), main(COL_CG), prev(COL_CU), prev(COL_CG), nxt(COL_CU), nxt(COL_CG), main(COL_CZ),
                  pl.BlockSpec((CONV_TAPS, w), lambda i: (0, 0)), vec, vec, vec,
                  br, br, gate(COL_GA), gate(COL_GB), gate(COL_GC), wspec, wspec, wspec],
        out_specs=pl.BlockSpec((tt, d), lambda i: (i, 0)),
        out_shape=jax.ShapeDtypeStruct((t_tok, d), BF16),
        scratch_shapes=[pltpu.VMEM((tt + 2 * CONV_HALO, w), F32), pltpu.VMEM((tt, w), F32)],
        compiler_params=_cparams("parallel"),
        name="conv_merge",
    )(p, p, p, p, p, p, p, conv_w, conv_b, norm_g, norm_b, og_a, og_b, p, p, p, wa, wb, wc)


OUT_SUB = 256


def _out_kernel(m_ref, w_ref, x_ref, mod_ref, g_ref, b_ref, o_ref):
    gate = mod_ref[0, :, 2 * D_MODEL:3 * D_MODEL]
    for r0 in range(0, m_ref.shape[0], OUT_SUB):
        rows = slice(r0, r0 + OUT_SUB)
        y = ALPHA * x_ref[rows, :] + gate * _dot(m_ref[rows, :], w_ref[...])
        mu = jnp.mean(y, -1, keepdims=True)
        yc = y - mu
        var = jnp.mean(yc * yc, -1, keepdims=True)
        o_ref[rows, :] = yc * lax.rsqrt(var + LN_EPS) * g_ref[...] + b_ref[...]


def _out_projection(merged, w_out, x2d, mod, ln_g, ln_b, seq_len):
    t_tok = x2d.shape[0]
    tm = 2 * OUT_SUB
    d = D_MODEL
    per = seq_len // tm if mod.shape[0] > 1 else t_tok // tm
    row = pl.BlockSpec((tm, d), lambda i: (i, 0))
    vec = pl.BlockSpec((1, d), lambda i: (0, 0))
    return pl.pallas_call(
        _out_kernel,
        grid=(t_tok // tm,),
        in_specs=[row, pl.BlockSpec((d, d), lambda i: (0, 0), pipeline_mode=pl.Buffered(1)), row,
                  pl.BlockSpec((1, 1, 3 * d), lambda i: (i // per, 0, 0)), vec, vec],
        out_specs=row,
        out_shape=jax.ShapeDtypeStruct((t_tok, d), F32),
        compiler_params=_cparams("parallel"),
        name="out_projection",
    )(merged, w_out, x2d, mod, ln_g, ln_b)


def kernel(x_prompt, x_sample, cache_k, cache_v, state_gla, c, c_ctx, w_mod, b_mod, w_in, rpb, gla_w_gate,
           gla_b_gate, gla_norm_g, conv_w, conv_b, conv_norm_g, conv_norm_b, w_proj_a, w_proj_b, w_proj_c,
           w_out, ln_g, ln_b):
    bsz, seq_len, d = x_prompt.shape
    dbsz, dseq, _ = x_sample.shape

    cvec = jnp.zeros((8, d), F32).at[0].set(c_ctx).at[1:1 + dbsz].set(c)
    mod_all = _modulation(cvec, w_mod, b_mod)

    w_in_t = jnp.swapaxes(w_in, 1, 2)
    w_t = _cast_rows_bf16(w_in_t)
    w_lr_t = jnp.pad(w_in_t[:, LR_START:LR_START + 2 * GLA_LOWRANK],
                     ((0, 0), (0, LANES - 2 * GLA_LOWRANK), (0, 0))).astype(BF16)
    wg = gla_w_gate.astype(BF16)
    wg_f = jnp.pad(wg[:, 0], ((0, 0), (0, LANES - GLA_LOWRANK), (0, 0)))
    wg_b = jnp.pad(wg[:, 1], ((0, 0), (GLA_LOWRANK, LANES - 2 * GLA_LOWRANK), (0, 0)))
    wpa, wpb, wpc, wo = (w.astype(BF16) for w in (w_proj_a, w_proj_b, w_proj_c, w_out))
    ck, cv = (a.reshape(a.shape[:3] + (NA_WIDTH,)) for a in (cache_k, cache_v))
    rope = _rope_tables(dseq)

    h_ctx = x_prompt.reshape(bsz * seq_len, d)
    h_lat = x_sample.reshape(dbsz * dseq, d)
    stacked = ()
    for l in range(DEPTH):
        mod_ctx = mod_all[l, 0:1][:, None, :]
        mod_lat = mod_all[l, 1:1 + dbsz][:, None, :]
        row = lambda a: a[l][None, :]
        gla_w = (wg_f[l], wg_b[l], gla_b_gate[l], row(gla_norm_g))
        conv_p = (conv_w[l], row(conv_b), row(conv_norm_g), row(conv_norm_b))

        def tail(x2d, p, og_a, og_b, mod, seq):
            merged = _conv_merge(p, og_a, og_b, *conv_p, wpa[l], wpb[l], wpc[l], seq)
            return _out_projection(merged, wo[l], x2d, mod, row(ln_g), row(ln_b), seq)

        p, p_lr = _in_projection(h_ctx, mod_ctx, w_t, w_lr_t, l, seq_len)
        og_a = _context_attention(p, bsz, seq_len)
        og_b, *stacked = _gla(p, p_lr, *gla_w, bsz, seq_len, l, state_prev=stacked)
        h_ctx = tail(h_ctx, p, og_a, og_b, mod_ctx, seq_len)

        p, p_lr = _in_projection(h_lat, mod_lat, w_t, w_lr_t, l, dseq)
        og_a = _neighbourhood_attention(p, ck, cv, rpb[l], l, dbsz, dseq)
        og_b = _gla(p, p_lr, *gla_w, dbsz, dseq, l, rope=rope, state=state_gla)
        h_lat = tail(h_lat, p, og_a, og_b, mod_lat, dseq)

    new_state, new_k, new_v = stacked
    new_k, new_v = (a.reshape(bsz, DEPTH, seq_len, NA_HEADS, NA_HEAD_DIM) for a in (new_k, new_v))
    return (h_ctx.reshape(bsz, seq_len, d), h_lat.reshape(dbsz, dseq, d), new_k, new_v, new_state)
```

```python
import functools

import numpy as np
import jax
import jax.numpy as jnp
from jax import lax
from jax.experimental import pallas as pl
from jax.experimental.pallas import tpu as pltpu

D_MODEL = 2048
DEPTH = 2
GRID_W = 64
NA_HEADS = 8
NA_HEAD_DIM = 128
NA_WIDTH = NA_HEADS * NA_HEAD_DIM
NA_WIN_H = 8
NA_WIN_W = 16
GLA_HEADS = 4
GLA_DK = 128
GLA_DV = 256
GLA_KW = GLA_HEADS * GLA_DK
GLA_VW = GLA_HEADS * GLA_DV
GLA_LOWRANK = 16
GLA_TAU = 16.0
GLA_CHUNK = 64
ROPE_BASE = 10000.0
CONV_WIDTH = 1024
CONV_TAPS = 31
ALPHA = (2 * DEPTH) ** 0.25
LN_EPS = 1e-5
NEG_INF = -1e30

LANES = 128
SUBLANES = 8
BF16_ROWS = 2 * SUBLANES
CONV_HALO = 16
VMEM_LIMIT = 56 * 1024 * 1024

COL_AQ, COL_AK, COL_AV, COL_AZ = 0, 1024, 2048, 3072
COL_BQ, COL_BK, COL_BV, COL_BZ = 4096, 4608, 5120, 6144
COL_CU, COL_CG, COL_CZ = 7168, 8192, 9216
COL_GA, COL_GB, COL_GC = 10240, 12288, 14336
N_MAIN = 16384
LR_START = 6144

BF16 = jnp.bfloat16
F32 = jnp.float32


def _cparams(*sem):
    return pltpu.CompilerParams(dimension_semantics=sem, vmem_limit_bytes=VMEM_LIMIT)


def _silu(x):
    return x * jax.nn.sigmoid(x)


def _dot(a, b):
    return jnp.dot(a, b, preferred_element_type=F32)


def _dot_nt(a, b):
    return lax.dot_general(a, b, (((1,), (1,)), ((), ())), preferred_element_type=F32)


def _mod_kernel(cv_ref, w_ref, b_ref, o_ref):
    s = _silu(cv_ref[...]).astype(BF16)
    o_ref[0] = _dot(s, w_ref[0].astype(BF16)) + b_ref[0]


def _modulation(cvec, w_mod, b_mod):
    tn = 1024
    n3 = 3 * D_MODEL
    return pl.pallas_call(
        _mod_kernel,
        grid=(DEPTH, n3 // tn),
        in_specs=[pl.BlockSpec((8, D_MODEL), lambda l, n: (0, 0)),
                  pl.BlockSpec((1, D_MODEL, tn), lambda l, n: (l, 0, n)),
                  pl.BlockSpec((1, 1, tn), lambda l, n: (l, 0, n))],
        out_specs=pl.BlockSpec((1, 8, tn), lambda l, n: (l, 0, n)),
        out_shape=jax.ShapeDtypeStruct((DEPTH, 8, n3), F32),
        compiler_params=_cparams("parallel", "parallel"),
        name="modulation",
    )(cvec, w_mod, b_mod.reshape(DEPTH, 1, n3))


def _cast_kernel(a_ref, o_ref):
    o_ref[...] = a_ref[...].astype(BF16)


def _cast_rows_bf16(w_t):
    depth, rows, d = w_t.shape
    tr = max(t for t in range(BF16_ROWS, 1024 + 1, BF16_ROWS) if rows % t == 0)
    spec = pl.BlockSpec((1, tr, d), lambda l, r: (l, r, 0))
    return pl.pallas_call(
        _cast_kernel,
        grid=(depth, rows // tr),
        in_specs=[spec],
        out_specs=spec,
        out_shape=jax.ShapeDtypeStruct(w_t.shape, BF16),
        compiler_params=_cparams("parallel", "parallel"),
        name="cast_w_in",
    )(w_t)


def _inproj_kernel(x_ref, mod_ref, w_ref, wlr_ref, o_ref, olr_ref, h_scr):
    strip = 256

    @pl.when(pl.program_id(1) == 0)
    def _():
        shift = mod_ref[0, :, 0:D_MODEL]
        scale = mod_ref[0, :, D_MODEL:2 * D_MODEL]

        def body(i, carry):
            sl = pl.ds(pl.multiple_of(i * strip, strip), strip)
            x = x_ref[sl, :]
            mu = jnp.mean(x, -1, keepdims=True)
            xc = x - mu
            var = jnp.mean(xc * xc, -1, keepdims=True)
            y = xc * lax.rsqrt(var + LN_EPS)
            hb = (y * (1.0 + scale) + shift).astype(BF16)
            h_scr[sl, :] = hb
            olr_ref[sl, :] = _dot_nt(hb, wlr_ref[...])
            return carry

        lax.fori_loop(0, x_ref.shape[0] // strip, body, 0)

    o_ref[...] = _dot_nt(h_scr[...], w_ref[0])


def _in_projection(x2d, mod, w_t, w_lr_t, layer, seq_len):
    t_tok = x2d.shape[0]
    tm, tn = 1024, 1024
    assert LR_START % tn == 0
    per = seq_len // tm if mod.shape[0] > 1 else t_tok // tm
    w_row = lambda n: pl.multiple_of(n * tn + jnp.where(n >= LR_START // tn, 2 * GLA_LOWRANK, 0), 2 * GLA_LOWRANK)
    return pl.pallas_call(
        _inproj_kernel,
        grid=(t_tok // tm, N_MAIN // tn),
        in_specs=[pl.BlockSpec((tm, D_MODEL), lambda m, n: (m, 0)),
                  pl.BlockSpec((1, 1, 3 * D_MODEL), lambda m, n: (m // per, 0, 0)),
                  pl.BlockSpec((pl.Element(1), pl.Element(tn), pl.Element(D_MODEL)),
                               lambda m, n: (layer, w_row(n), 0)),
                  pl.BlockSpec((None, LANES, D_MODEL), lambda m, n: (layer, 0, 0))],
        out_specs=[pl.BlockSpec((tm, tn), lambda m, n: (m, n)),
                   pl.BlockSpec((tm, LANES), lambda m, n: (m, 0))],
        out_shape=[jax.ShapeDtypeStruct((t_tok, N_MAIN), F32),
                   jax.ShapeDtypeStruct((t_tok, LANES), F32)],
        scratch_shapes=[pltpu.VMEM((tm, D_MODEL), BF16)],
        compiler_params=_cparams("parallel", "arbitrary"),
        name="in_projection",
    )(x2d, mod, w_t, w_lr_t)


def _ctx_attn_kernel(q_ref, k_ref, v_ref, z_ref, o_ref):
    scale = NA_HEAD_DIM ** -0.5
    for h in range(NA_HEADS):
        cs = slice(h * NA_HEAD_DIM, (h + 1) * NA_HEAD_DIM)
        q = q_ref[:, cs].astype(BF16)
        k = k_ref[:, cs].astype(BF16)
        v = v_ref[:, cs].astype(BF16)
        s = _dot_nt(q, k) * scale
        e = jnp.exp(s - jnp.max(s, -1, keepdims=True))
        p = e / jnp.sum(e, -1, keepdims=True)
        o = _dot(p.astype(BF16), v)
        o_ref[:, cs] = (o * _silu(z_ref[:, cs])).astype(BF16)


def _context_attention(p_ctx, bsz, seq_len):
    blk = lambda j: pl.BlockSpec((seq_len, NA_WIDTH), lambda b: (b, j))
    return pl.pallas_call(
        _ctx_attn_kernel,
        grid=(bsz,),
        in_specs=[blk(COL_AQ // NA_WIDTH), blk(COL_AK // NA_WIDTH), blk(COL_AV // NA_WIDTH), blk(COL_AZ // NA_WIDTH)],
        out_specs=pl.BlockSpec((seq_len, NA_WIDTH), lambda b: (b, 0)),
        out_shape=jax.ShapeDtypeStruct((bsz * seq_len, NA_WIDTH), BF16),
        compiler_params=_cparams("parallel"),
        name="context_attention",
    )(p_ctx, p_ctx, p_ctx, p_ctx)


NA_QROWS = 4
NA_KROWS = NA_QROWS + NA_WIN_H


NA_NDR = 2 * NA_WIN_H


def _na_plan(rows):
    kh = min(NA_WIN_H, rows)
    masked = NA_NDR - 1
    groups = []
    for r0 in range(0, rows, NA_QROWS):
        ks = int(np.clip(r0 - kh // 2, 0, rows - NA_KROWS))
        tiles = []
        for rq in range(NA_QROWS):
            r = r0 + rq
            rs = int(np.clip(r - kh // 2, 0, rows - kh))
            slot = [ks + i - r + NA_WIN_H - 1 if rs <= ks + i < rs + kh else masked for i in range(NA_KROWS)]
            row_tiles = []
            for i in range(0, NA_KROWS, 2):
                a, b = slot[i], slot[i + 1]
                if a != masked and b != masked:
                    row_tiles.append(b)
                elif b != masked:
                    row_tiles.append(NA_NDR + b)
                else:
                    row_tiles.append(2 * NA_NDR + a)
            tiles.append(tuple(row_tiles))
        groups.append((r0, ks, tuple(tiles)))
    return tuple(groups)


def _na_bias_tiles(rpb_l):
    nh = rpb_l.shape[0]
    n_dr = 2 * NA_WIN_H - 1
    c = np.arange(GRID_W)
    cs = np.clip(c - NA_WIN_W // 2, 0, GRID_W - NA_WIN_W)
    valid = (c[None, :] >= cs[:, None]) & (c[None, :] < cs[:, None] + NA_WIN_W)
    span = 2 * GRID_W
    left = (GRID_W - 1) - (NA_WIN_W - 1)
    wv = jnp.pad(rpb_l.astype(F32), ((0, 0), (0, 0), (left, span - left - (2 * NA_WIN_W - 1))))
    skew = jnp.tile(wv, (1, 1, GRID_W))[..., :GRID_W * (span - 1)].reshape(nh, n_dr, GRID_W, span - 1)
    toe = jnp.where(valid, skew[..., GRID_W - 1:], NEG_INF)
    neg1 = jnp.full((nh, 1, GRID_W, GRID_W), NEG_INF, F32)
    negs = jnp.full((nh, NA_NDR, GRID_W, GRID_W), NEG_INF, F32)
    cur = jnp.concatenate([toe, neg1], axis=1)
    prev = jnp.concatenate([neg1, toe], axis=1)
    return jnp.concatenate([jnp.concatenate([prev, cur], -1), jnp.concatenate([negs, cur], -1),
                            jnp.concatenate([cur, negs], -1)], axis=1)


def _na_kernel(q_ref, k_ref, v_ref, z_ref, kc_ref, vc_ref, bias_ref, o_ref, kb_scr, vb_scr, *, groups):
    scale = NA_HEAD_DIM ** -0.5
    kb_scr[...] = k_ref[...].astype(BF16)
    vb_scr[...] = v_ref[...].astype(BF16)
    kctx = kc_ref[...].astype(BF16)
    vctx = vc_ref[...].astype(BF16)
    nq, nk = NA_QROWS * GRID_W, NA_KROWS * GRID_W
    for r0, ks, tiles in groups:
        qsl = slice(r0 * GRID_W, r0 * GRID_W + nq)
        ksl = slice(ks * GRID_W, ks * GRID_W + nk)
        q = q_ref[qsl, :].astype(BF16)
        bias = jnp.concatenate([jnp.concatenate([bias_ref[t] for t in row], axis=1) for row in tiles], axis=0)
        s_win = _dot_nt(q, kb_scr[ksl, :]) * scale + bias
        s_ctx = _dot_nt(q, kctx) * scale
        m = jnp.maximum(jnp.max(s_win, -1, keepdims=True), jnp.max(s_ctx, -1, keepdims=True))
        e_win = jnp.exp(s_win - m)
        e_ctx = jnp.exp(s_ctx - m)
        den = jnp.sum(e_win, -1, keepdims=True) + jnp.sum(e_ctx, -1, keepdims=True)
        o = _dot((e_win / den).astype(BF16), vb_scr[ksl, :]) + _dot((e_ctx / den).astype(BF16), vctx)
        o_ref[qsl, :] = (o * _silu(z_ref[qsl, :])).astype(BF16)


def _neighbourhood_attention(p_lat, cache_k, cache_v, rpb_l, layer, bsz, n_tok):
    rows = n_tok // GRID_W
    assert rows % NA_QROWS == 0 and rows >= NA_KROWS and NA_KROWS % 2 == 0
    groups = _na_plan(rows)
    bias_tab = _na_bias_tiles(rpb_l)
    past = cache_k.shape[2]
    hd = NA_HEAD_DIM
    blk = lambda col: pl.BlockSpec((n_tok, hd), lambda b, h: (b, col // hd + h))
    cblk = pl.BlockSpec((None, None, past, hd), lambda b, h: (b, layer, 0, h))
    return pl.pallas_call(
        functools.partial(_na_kernel, groups=groups),
        grid=(bsz, NA_HEADS),
        in_specs=[blk(COL_AQ), blk(COL_AK), blk(COL_AV), blk(COL_AZ), cblk, cblk,
                  pl.BlockSpec((None,) + bias_tab.shape[1:], lambda b, h: (h, 0, 0, 0))],
        out_specs=pl.BlockSpec((n_tok, hd), lambda b, h: (b, h)),
        out_shape=jax.ShapeDtypeStruct((bsz * n_tok, NA_WIDTH), BF16),
        scratch_shapes=[pltpu.VMEM((n_tok, hd), BF16), pltpu.VMEM((n_tok, hd), BF16)],
        compiler_params=_cparams("parallel", "parallel"),
        name="neighbourhood_attention",
    )(p_lat, p_lat, p_lat, p_lat, cache_k, cache_v, bias_tab)


def _rope_tables(seq_len):
    t = jnp.arange(seq_len)
    half = GLA_DK // 2
    inv = ROPE_BASE ** (-jnp.arange(0, half, 2, dtype=F32) / half)

    def tab(pos):
        ang = pos.astype(F32)[:, None] * inv[None, :]
        return jnp.cos(ang), jnp.sin(ang)

    cr, sr = tab(t // GRID_W)
    cc, sc = tab(t % GRID_W)
    cos = jnp.concatenate([cr, cr, cc, cc], -1)
    sin = jnp.concatenate([-sr, sr, -sc, sc], -1)
    return cos, sin


def _rope(x, cos, sin):
    quarter = GLA_DK // 4
    lane = lax.broadcasted_iota(jnp.int32, x.shape, 1)
    partner = jnp.where((lane % (2 * quarter)) < quarter,
                        pltpu.roll(x, GLA_DK - quarter, 1), pltpu.roll(x, quarter, 1))
    return x * cos + partner * sin


def _log_sigmoid(x):
    return jnp.minimum(x, 0.0) - jnp.log1p(jnp.exp(-jnp.abs(x)))


def _chunk_scan(x, reverse):
    n = x.shape[0]
    pos = lax.broadcasted_iota(jnp.int32, x.shape, 0) % GLA_CHUNK
    step = 1
    while step < GLA_CHUNK:
        if reverse:
            x = x + jnp.where(pos < GLA_CHUNK - step, pltpu.roll(x, n - step, 0), 0.0)
        else:
            x = x + jnp.where(pos >= step, pltpu.roll(x, step, 0), 0.0)
        step *= 2
    return x


GLA_GROUP = 4


def _gla_kernel(*refs, latent, n_groups, heads, layer):
    (q_ref, k_ref, v_ref, z_ref, lr_ref, wgf_ref, wgb_ref, bg_ref, ng_ref), rest = refs[:9], refs[9:]
    if latent:
        cos_ref, sin_ref, s0_ref, o_ref, qs, ks, laf, lab, o_f, o_b, st_f, st_b = rest
    else:
        ak_ref, av_ref = rest[:2]
        o_ref, sfin_ref, ko_ref, vo_ref, qs, ks, laf, lab, o_f, o_b, st_f, st_b = rest[-12:]
        if len(rest) == 15:
            for l in range(DEPTH):
                if l != layer:
                    sfin_ref[l] = jnp.zeros(sfin_ref.shape[1:], F32)
                    ko_ref[l] = jnp.zeros(ko_ref.shape[1:], F32)
                    vo_ref[l] = jnp.zeros(vo_ref.shape[1:], F32)
            sfin_ref, ko_ref, vo_ref = sfin_ref.at[layer], ko_ref.at[layer], vo_ref.at[layer]
        ko_ref[...] = ak_ref[...]
        vo_ref[...] = av_ref[...]
    c_len = GLA_CHUNK
    g_len = GLA_GROUP * c_len
    dk, dv = GLA_DK, GLA_DV

    q = q_ref[...] * (GLA_DK ** -0.5)
    k = k_ref[...]
    if latent:
        q = _rope(q, cos_ref[...], sin_ref[...])
        k = _rope(k, cos_ref[...], sin_ref[...])
    qs[...] = q
    ks[...] = k
    lrb = lr_ref[...].astype(BF16)
    laf[...] = _log_sigmoid(_dot(lrb, wgf_ref[...]) + bg_ref[0:1, :]) * (1.0 / GLA_TAU)
    lab[...] = _log_sigmoid(_dot(lrb, wgb_ref[...]) + bg_ref[1:2, :]) * (1.0 / GLA_TAU)

    ii = lax.broadcasted_iota(jnp.int32, (c_len, c_len), 0)
    jj = lax.broadcasted_iota(jnp.int32, (c_len, c_len), 1)

    def half_group(g, reverse, hh):
        la_ref, out, st = (lab, o_b, st_b) if reverse else (laf, o_f, st_f)
        kcols, vcols, srows = slice(hh * dk, (hh + 1) * dk), slice(hh * dv, (hh + 1) * dv), slice(hh * dv, (hh + 1) * dv)
        mask = (ii <= jj) if reverse else (ii >= jj)
        edge = 0 if reverse else c_len - 1
        start = g * g_len if isinstance(g, int) else pl.multiple_of(g * g_len, g_len)
        b_all = _chunk_scan(la_ref[pl.ds(start, g_len), kcols], reverse)
        s_t = st[srows, :]
        for c in (range(GLA_GROUP - 1, -1, -1) if reverse else range(GLA_GROUP)):
            sl = pl.ds(start + c * c_len, c_len)
            b = b_all[c * c_len:(c + 1) * c_len]
            btot = b[edge:edge + 1, :]
            qc = qs[sl, kcols]
            kc = ks[sl, kcols]
            vc = v_ref[sl, vcols]
            qe = (qc * jnp.exp(b)).astype(BF16)
            ke = (kc * jnp.exp(-b)).astype(BF16)
            kw = (kc * jnp.exp(btot - b)).astype(BF16)
            att = jnp.where(mask, _dot_nt(qe, ke), 0.0)
            out[sl, vcols] = _dot(att.astype(BF16), vc.astype(BF16)) + _dot_nt(qe, s_t.astype(BF16))
            s_t = jnp.exp(btot) * s_t + _dot(vc.T.astype(BF16), kw)
        st[srows, :] = s_t

    for direction, st in enumerate((st_f, st_b)):
        st[...] = s0_ref[direction].T if latent else jnp.zeros_like(st)
    if n_groups == 1:
        for hh in range(heads):
            half_group(0, False, hh)
            half_group(0, True, hh)
    else:
        def body(i, carry):
            for hh in range(heads):
                half_group(i, False, hh)
                half_group(n_groups - 1 - i, True, hh)
            return carry
        lax.fori_loop(0, n_groups, body, 0)

    for hh in range(heads):
        kcols, vcols = slice(hh * dk, (hh + 1) * dk), slice(hh * dv, (hh + 1) * dv)
        if not latent:
            sfin_ref[0, hh] = st_f[vcols, :].T
            sfin_ref[1, hh] = st_b[vcols, :].T
        o = (o_f[:, vcols] + o_b[:, vcols]) - jnp.sum(qs[:, kcols] * ks[:, kcols], -1, keepdims=True) * v_ref[:, vcols]
        o = o * lax.rsqrt(jnp.mean(o * o, -1, keepdims=True) + LN_EPS) * ng_ref[:, vcols]
        o_ref[:, vcols] = (o * _silu(z_ref[:, vcols])).astype(BF16)


def _gla(p, p_lr, wg_f, wg_b, b_gate, norm_g, bsz, seq_len, layer, rope=None, state=None, state_prev=(), after=None):
    latent = state is not None
    assert seq_len % (GLA_GROUP * GLA_CHUNK) == 0
    dk, dv = GLA_DK, GLA_DV
    hps = 1 if latent else GLA_HEADS
    kw, vw = hps * dk, hps * dv
    in_specs = [pl.BlockSpec((seq_len, kw), lambda b, h: (b, COL_BQ // kw + h)),
                pl.BlockSpec((seq_len, kw), lambda b, h: (b, COL_BK // kw + h)),
                pl.BlockSpec((seq_len, vw), lambda b, h: (b, COL_BV // vw + h)),
                pl.BlockSpec((seq_len, vw), lambda b, h: (b, COL_BZ // vw + h)),
                pl.BlockSpec((seq_len, LANES), lambda b, h: (b, 0)),
                pl.BlockSpec((LANES, kw), lambda b, h: (0, h)),
                pl.BlockSpec((LANES, kw), lambda b, h: (0, h)),
                pl.BlockSpec((2, kw), lambda b, h: (0, h)),
                pl.BlockSpec((1, vw), lambda b, h: (0, h))]
    args = [p, p, p, p, p_lr, wg_f, wg_b, b_gate, norm_g]
    o_spec = pl.BlockSpec((seq_len, vw), lambda b, h: (b, h))
    o_shape = jax.ShapeDtypeStruct((bsz * seq_len, GLA_VW), BF16)
    if latent:
        cos, sin = rope
        in_specs += [pl.BlockSpec((seq_len, dk), lambda b, h: (0, 0)),
                     pl.BlockSpec((seq_len, dk), lambda b, h: (0, 0)),
                     pl.BlockSpec((None, None, 2, None, dk, dv), lambda b, h: (b, layer, 0, h, 0, 0))]
        args += [cos, sin, state]
        out_specs, out_shape, aliases = o_spec, o_shape, {}
    else:
        assert hps == GLA_HEADS
        in_specs += [pl.BlockSpec((seq_len, NA_WIDTH), lambda b, h: (b, COL_AK // NA_WIDTH)),
                     pl.BlockSpec((seq_len, NA_WIDTH), lambda b, h: (b, COL_AV // NA_WIDTH)),
                     pl.BlockSpec(memory_space=pl.ANY)]
        args += [p, p, after]
        aliases = {len(args) + i: 1 + i for i in range(len(state_prev))}
        in_specs += [pl.BlockSpec(memory_space=pl.ANY)] * len(state_prev)
        args += list(state_prev)
        if state_prev:
            s_spec = pl.BlockSpec((None, None, 2, hps, dk, dv), lambda b, h: (b, layer, 0, h, 0, 0))
            kv_spec = pl.BlockSpec((None, None, seq_len, NA_WIDTH), lambda b, h: (b, layer, 0, 0))
        else:
            s_spec = pl.BlockSpec((None, DEPTH, 2, hps, dk, dv), lambda b, h: (b, 0, 0, h, 0, 0))
            kv_spec = pl.BlockSpec((None, DEPTH, seq_len, NA_WIDTH), lambda b, h: (b, 0, 0, 0))
        kv_shape = jax.ShapeDtypeStruct((bsz, DEPTH, seq_len, NA_WIDTH), F32)
        out_specs = [o_spec, s_spec, kv_spec, kv_spec]
        out_shape = [o_shape, jax.ShapeDtypeStruct((bsz, DEPTH, 2, GLA_HEADS, dk, dv), F32), kv_shape, kv_shape]
    return pl.pallas_call(
        functools.partial(_gla_kernel, latent=latent, n_groups=seq_len // (GLA_GROUP * GLA_CHUNK), heads=hps,
                          layer=layer),
        grid=(bsz, GLA_HEADS // hps),
        in_specs=in_specs,
        out_specs=out_specs,
        out_shape=out_shape,
        input_output_aliases=aliases,
        scratch_shapes=[pltpu.VMEM((seq_len, kw), F32), pltpu.VMEM((seq_len, kw), F32),
                        pltpu.VMEM((seq_len, kw), F32), pltpu.VMEM((seq_len, kw), F32),
                        pltpu.VMEM((seq_len, vw), F32), pltpu.VMEM((seq_len, vw), F32),
                        pltpu.VMEM((vw, dk), F32), pltpu.VMEM((vw, dk), F32)],
        compiler_params=_cparams("parallel", "parallel"),
        name="gla_latent" if latent else "gla_context",
    )(*args)


def _conv_merge_kernel(u_ref, g_ref, up_ref, gp_ref, un_ref, gn_ref, z_ref, w_ref, b_ref, ng_ref, nb_ref,
                       oa_ref, ob_ref, ga_ref, gb_ref, gc_ref, wa_ref, wb_ref, wc_ref, o_ref, pad, cv,
                       *, tiles_per_seq, tt):
    m = jax.nn.sigmoid(ga_ref[...]) * _dot(oa_ref[...], wa_ref[...])
    m = m + jax.nn.sigmoid(gb_ref[...]) * _dot(ob_ref[...], wb_ref[...])
    t = pl.program_id(0) % tiles_per_seq
    halo = CONV_HALO
    glu = lambda u, g: u * jax.nn.sigmoid(g)
    pad[halo:halo + tt, :] = glu(u_ref[...], g_ref[...])
    pad[0:halo, :] = jnp.where(t > 0, glu(up_ref[...], gp_ref[...]), 0.0)
    pad[halo + tt:halo + tt + halo, :] = jnp.where(t < tiles_per_seq - 1, glu(un_ref[...], gn_ref[...]), 0.0)
    first = halo - CONV_TAPS // 2
    rb = 64
    reach = -(-(first + CONV_TAPS - 1) // SUBLANES) * SUBLANES
    for c0 in range(0, CONV_WIDTH, LANES):
        lanes = slice(c0, c0 + LANES)
        w_strip = w_ref[:, lanes]
        for r0 in range(0, tt, rb):
            base = pad[r0:r0 + rb + reach, lanes]
            acc = jnp.broadcast_to(b_ref[:, lanes], (rb, LANES))
            for phase in range(SUBLANES):
                rows = pltpu.roll(base, rb + reach - phase, 0) if phase else base
                for a in range(0, reach, SUBLANES):
                    j = a + phase - first
                    if 0 <= j < CONV_TAPS:
                        acc = acc + rows[a:a + rb] * w_strip[j:j + 1, :]
            cv[r0:r0 + rb, lanes] = acc
    acc = cv[...]
    mu = jnp.mean(acc, -1, keepdims=True)
    xc = acc - mu
    var = jnp.mean(xc * xc, -1, keepdims=True)
    y = xc * lax.rsqrt(var + LN_EPS) * ng_ref[...] + nb_ref[...]
    oc = (_silu(y) * _silu(z_ref[...])).astype(BF16)
    m = m + jax.nn.sigmoid(gc_ref[...]) * _dot(oc, wc_ref[...])
    o_ref[...] = m.astype(BF16)


def _conv_merge(p, og_a, og_b, conv_w, conv_b, norm_g, norm_b, wa, wb, wc, seq_len):
    t_tok = p.shape[0]
    tt = 256
    tiles_per_seq = seq_len // tt
    w = CONV_WIDTH
    d = D_MODEL
    hb = tt // CONV_HALO
    n_halo = t_tok // CONV_HALO
    main = lambda col: pl.BlockSpec((tt, w), lambda i: (i, col // w))
    prev = lambda col: pl.BlockSpec((CONV_HALO, w), lambda i: (jnp.maximum(i * hb - 1, 0), col // w))
    nxt = lambda col: pl.BlockSpec((CONV_HALO, w), lambda i: (jnp.minimum((i + 1) * hb, n_halo - 1), col // w))
    vec = pl.BlockSpec((1, w), lambda i: (0, 0))
    br = pl.BlockSpec((tt, w), lambda i: (i, 0))
    gate = lambda col: pl.BlockSpec((tt, d), lambda i: (i, col // d))
    wspec = pl.BlockSpec((w, d), lambda i: (0, 0), pipeline_mode=pl.Buffered(1))
    return pl.pallas_call(
        functools.partial(_conv_merge_kernel, tiles_per_seq=tiles_per_seq, tt=tt),
        grid=(t_tok // tt,),
        in_specs=[main(COL_CU), main(COL_CG), prev(COL_CU), prev(COL_CG), nxt(COL_CU), nxt(COL_CG), main(COL_CZ),
                  pl.BlockSpec((CONV_TAPS, w), lambda i: (0, 0)), vec, vec, vec,
                  br, br, gate(COL_GA), gate(COL_GB), gate(COL_GC), wspec, wspec, wspec],
        out_specs=pl.BlockSpec((tt, d), lambda i: (i, 0)),
        out_shape=jax.ShapeDtypeStruct((t_tok, d), BF16),
        scratch_shapes=[pltpu.VMEM((tt + 2 * CONV_HALO, w), F32), pltpu.VMEM((tt, w), F32)],
        compiler_params=_cparams("parallel"),
        name="conv_merge",
    )(p, p, p, p, p, p, p, conv_w, conv_b, norm_g, norm_b, og_a, og_b, p, p, p, wa, wb, wc)


OUT_SUB = 256


def _out_kernel(m_ref, w_ref, x_ref, mod_ref, g_ref, b_ref, o_ref):
    gate = mod_ref[0, :, 2 * D_MODEL:3 * D_MODEL]
    for r0 in range(0, m_ref.shape[0], OUT_SUB):
        rows = slice(r0, r0 + OUT_SUB)
        y = ALPHA * x_ref[rows, :] + gate * _dot(m_ref[rows, :], w_ref[...])
        mu = jnp.mean(y, -1, keepdims=True)
        yc = y - mu
        var = jnp.mean(yc * yc, -1, keepdims=True)
        o_ref[rows, :] = yc * lax.rsqrt(var + LN_EPS) * g_ref[...] + b_ref[...]


def _out_projection(merged, w_out, x2d, mod, ln_g, ln_b, seq_len):
    t_tok = x2d.shape[0]
    tm = 2 * OUT_SUB
    d = D_MODEL
    per = seq_len // tm if mod.shape[0] > 1 else t_tok // tm
    row = pl.BlockSpec((tm, d), lambda i: (i, 0))
    vec = pl.BlockSpec((1, d), lambda i: (0, 0))
    return pl.pallas_call(
        _out_kernel,
        grid=(t_tok // tm,),
        in_specs=[row, pl.BlockSpec((d, d), lambda i: (0, 0), pipeline_mode=pl.Buffered(1)), row,
                  pl.BlockSpec((1, 1, 3 * d), lambda i: (i // per, 0, 0)), vec, vec],
        out_specs=row,
        out_shape=jax.ShapeDtypeStruct((t_tok, d), F32),
        compiler_params=_cparams("parallel"),
        name="out_projection",
    )(merged, w_out, x2d, mod, ln_g, ln_b)


def kernel(x_prompt, x_sample, cache_k, cache_v, state_gla, c, c_ctx, w_mod, b_mod, w_in, rpb, gla_w_gate,
           gla_b_gate, gla_norm_g, conv_w, conv_b, conv_norm_g, conv_norm_b, w_proj_a, w_proj_b, w_proj_c,
           w_out, ln_g, ln_b):
    bsz, seq_len, d = x_prompt.shape
    dbsz, dseq, _ = x_sample.shape

    cvec = jnp.zeros((8, d), F32).at[0].set(c_ctx).at[1:1 + dbsz].set(c)
    mod_all = _modulation(cvec, w_mod, b_mod)

    w_in_t = jnp.swapaxes(w_in, 1, 2)
    w_t = _cast_rows_bf16(w_in_t)
    w_lr_t = jnp.pad(w_in_t[:, LR_START:LR_START + 2 * GLA_LOWRANK],
                     ((0, 0), (0, LANES - 2 * GLA_LOWRANK), (0, 0))).astype(BF16)
    wg = gla_w_gate.astype(BF16)
    wg_f = jnp.pad(wg[:, 0], ((0, 0), (0, LANES - GLA_LOWRANK), (0, 0)))
    wg_b = jnp.pad(wg[:, 1], ((0, 0), (GLA_LOWRANK, LANES - 2 * GLA_LOWRANK), (0, 0)))
    wpa, wpb, wpc, wo = (w.astype(BF16) for w in (w_proj_a, w_proj_b, w_proj_c, w_out))
    ck, cv = (a.reshape(a.shape[:3] + (NA_WIDTH,)) for a in (cache_k, cache_v))
    rope = _rope_tables(dseq)

    h_ctx = x_prompt.reshape(bsz * seq_len, d)
    h_lat = x_sample.reshape(dbsz * dseq, d)
    stacked = ()
    for l in range(DEPTH):
        mod_ctx = mod_all[l, 0:1][:, None, :]
        mod_lat = mod_all[l, 1:1 + dbsz][:, None, :]
        row = lambda a: a[l][None, :]
        gla_w = (wg_f[l], wg_b[l], gla_b_gate[l], row(gla_norm_g))
        conv_p = (conv_w[l], row(conv_b), row(conv_norm_g), row(conv_norm_b))

        def tail(x2d, p, og_a, og_b, mod, seq):
            merged = _conv_merge(p, og_a, og_b, *conv_p, wpa[l], wpb[l], wpc[l], seq)
            return _out_projection(merged, wo[l], x2d, mod, row(ln_g), row(ln_b), seq)

        p, p_lr = _in_projection(h_ctx, mod_ctx, w_t, w_lr_t, l, seq_len)
        og_a = _context_attention(p, bsz, seq_len)
        og_b, *stacked = _gla(p, p_lr, *gla_w, bsz, seq_len, l, state_prev=stacked, after=og_a)
        h_ctx = tail(h_ctx, p, og_a, og_b, mod_ctx, seq_len)

        p, p_lr = _in_projection(h_lat, mod_lat, w_t, w_lr_t, l, dseq)
        og_a = _neighbourhood_attention(p, ck, cv, rpb[l], l, dbsz, dseq)
        og_b = _gla(p, p_lr, *gla_w, dbsz, dseq, l, rope=rope, state=state_gla)
        h_lat = tail(h_lat, p, og_a, og_b, mod_lat, dseq)

    new_state, new_k, new_v = stacked
    new_k, new_v = (a.reshape(bsz, DEPTH, seq_len, NA_HEADS, NA_HEAD_DIM) for a in (new_k, new_v))
    return (h_ctx.reshape(bsz, seq_len, d), h_lat.reshape(dbsz, dseq, d), new_k, new_v, new_state)
```

```python
import functools

import numpy as np
import jax
import jax.numpy as jnp
from jax import lax
from jax.experimental import pallas as pl
from jax.experimental.pallas import tpu as pltpu

D_MODEL = 2048
DEPTH = 2
GRID_W = 64
NA_HEADS = 8
NA_HEAD_DIM = 128
NA_WIDTH = NA_HEADS * NA_HEAD_DIM
NA_WIN_H = 8
NA_WIN_W = 16
GLA_HEADS = 4
GLA_DK = 128
GLA_DV = 256
GLA_KW = GLA_HEADS * GLA_DK
GLA_VW = GLA_HEADS * GLA_DV
GLA_LOWRANK = 16
GLA_TAU = 16.0
GLA_CHUNK = 64
ROPE_BASE = 10000.0
CONV_WIDTH = 1024
CONV_TAPS = 31
ALPHA = (2 * DEPTH) ** 0.25
LN_EPS = 1e-5
NEG_INF = -1e30

LANES = 128
SUBLANES = 8
BF16_ROWS = 2 * SUBLANES
CONV_HALO = 16
VMEM_LIMIT = 60 * 1024 * 1024

COL_AQ, COL_AK, COL_AV, COL_AZ = 0, 1024, 2048, 3072
COL_BQ, COL_BK, COL_BV, COL_BZ = 4096, 4608, 5120, 6144
COL_CU, COL_CG, COL_CZ = 7168, 8192, 9216
COL_GA, COL_GB, COL_GC = 10240, 12288, 14336
N_MAIN = 16384
LR_START = 6144

BF16 = jnp.bfloat16
F32 = jnp.float32


def _cparams(*sem):
    return pltpu.CompilerParams(dimension_semantics=sem, vmem_limit_bytes=VMEM_LIMIT)


def _silu(x):
    return x * jax.nn.sigmoid(x)


def _dot(a, b):
    return jnp.dot(a, b, preferred_element_type=F32)


def _dot_nt(a, b):
    return lax.dot_general(a, b, (((1,), (1,)), ((), ())), preferred_element_type=F32)


def _mod_kernel(cv_ref, w_ref, b_ref, o_ref):
    s = _silu(cv_ref[...]).astype(BF16)
    o_ref[0] = _dot(s, w_ref[0].astype(BF16)) + b_ref[0]


def _modulation(cvec, w_mod, b_mod):
    tn = 1024
    n3 = 3 * D_MODEL
    return pl.pallas_call(
        _mod_kernel,
        grid=(DEPTH, n3 // tn),
        in_specs=[pl.BlockSpec((8, D_MODEL), lambda l, n: (0, 0)),
                  pl.BlockSpec((1, D_MODEL, tn), lambda l, n: (l, 0, n)),
                  pl.BlockSpec((1, 1, tn), lambda l, n: (l, 0, n))],
        out_specs=pl.BlockSpec((1, 8, tn), lambda l, n: (l, 0, n)),
        out_shape=jax.ShapeDtypeStruct((DEPTH, 8, n3), F32),
        compiler_params=_cparams("parallel", "parallel"),
        name="modulation",
    )(cvec, w_mod, b_mod.reshape(DEPTH, 1, n3))


def _cast_kernel(a_ref, o_ref):
    o_ref[...] = a_ref[...].astype(BF16)


def _cast_rows_bf16(w_t):
    depth, rows, d = w_t.shape
    tr = max(t for t in range(BF16_ROWS, 1024 + 1, BF16_ROWS) if rows % t == 0)
    spec = pl.BlockSpec((1, tr, d), lambda l, r: (l, r, 0))
    return pl.pallas_call(
        _cast_kernel,
        grid=(depth, rows // tr),
        in_specs=[spec],
        out_specs=spec,
        out_shape=jax.ShapeDtypeStruct(w_t.shape, BF16),
        compiler_params=_cparams("parallel", "parallel"),
        name="cast_w_in",
    )(w_t)


def _inproj_kernel(x_ref, mod_ref, w_ref, wlr_ref, o_ref, olr_ref, h_scr):
    strip = 256

    @pl.when(pl.program_id(1) == 0)
    def _():
        shift = mod_ref[0, :, 0:D_MODEL]
        scale = mod_ref[0, :, D_MODEL:2 * D_MODEL]

        def body(i, carry):
            sl = pl.ds(pl.multiple_of(i * strip, strip), strip)
            x = x_ref[sl, :]
            mu = jnp.mean(x, -1, keepdims=True)
            xc = x - mu
            var = jnp.mean(xc * xc, -1, keepdims=True)
            y = xc * lax.rsqrt(var + LN_EPS)
            hb = (y * (1.0 + scale) + shift).astype(BF16)
            h_scr[sl, :] = hb
            olr_ref[sl, :] = _dot_nt(hb, wlr_ref[...])
            return carry

        lax.fori_loop(0, x_ref.shape[0] // strip, body, 0)

    o_ref[...] = _dot_nt(h_scr[...], w_ref[0])


def _in_projection(x2d, mod, w_t, w_lr_t, layer, seq_len):
    t_tok = x2d.shape[0]
    tm, tn = 1024, 2048
    assert LR_START % tn == 0
    per = seq_len // tm if mod.shape[0] > 1 else t_tok // tm
    w_row = lambda n: pl.multiple_of(n * tn + jnp.where(n >= LR_START // tn, 2 * GLA_LOWRANK, 0), 2 * GLA_LOWRANK)
    return pl.pallas_call(
        _inproj_kernel,
        grid=(t_tok // tm, N_MAIN // tn),
        in_specs=[pl.BlockSpec((tm, D_MODEL), lambda m, n: (m, 0)),
                  pl.BlockSpec((1, 1, 3 * D_MODEL), lambda m, n: (m // per, 0, 0)),
                  pl.BlockSpec((pl.Element(1), pl.Element(tn), pl.Element(D_MODEL)),
                               lambda m, n: (layer, w_row(n), 0)),
                  pl.BlockSpec((None, LANES, D_MODEL), lambda m, n: (layer, 0, 0))],
        out_specs=[pl.BlockSpec((tm, tn), lambda m, n: (m, n)),
                   pl.BlockSpec((tm, LANES), lambda m, n: (m, 0))],
        out_shape=[jax.ShapeDtypeStruct((t_tok, N_MAIN), F32),
                   jax.ShapeDtypeStruct((t_tok, LANES), F32)],
        scratch_shapes=[pltpu.VMEM((tm, D_MODEL), BF16)],
        compiler_params=_cparams("parallel", "arbitrary"),
        name="in_projection",
    )(x2d, mod, w_t, w_lr_t)


def _ctx_attn_kernel(q_ref, k_ref, v_ref, z_ref, o_ref):
    scale = NA_HEAD_DIM ** -0.5
    for h in range(NA_HEADS):
        cs = slice(h * NA_HEAD_DIM, (h + 1) * NA_HEAD_DIM)
        q = q_ref[:, cs].astype(BF16)
        k = k_ref[:, cs].astype(BF16)
        v = v_ref[:, cs].astype(BF16)
        s = _dot_nt(q, k) * scale
        e = jnp.exp(s - jnp.max(s, -1, keepdims=True))
        p = e / jnp.sum(e, -1, keepdims=True)
        o = _dot(p.astype(BF16), v)
        o_ref[:, cs] = (o * _silu(z_ref[:, cs])).astype(BF16)


def _context_attention(p_ctx, bsz, seq_len):
    blk = lambda j: pl.BlockSpec((seq_len, NA_WIDTH), lambda b: (b, j))
    return pl.pallas_call(
        _ctx_attn_kernel,
        grid=(bsz,),
        in_specs=[blk(COL_AQ // NA_WIDTH), blk(COL_AK // NA_WIDTH), blk(COL_AV // NA_WIDTH), blk(COL_AZ // NA_WIDTH)],
        out_specs=pl.BlockSpec((seq_len, NA_WIDTH), lambda b: (b, 0)),
        out_shape=jax.ShapeDtypeStruct((bsz * seq_len, NA_WIDTH), BF16),
        compiler_params=_cparams("parallel"),
        name="context_attention",
    )(p_ctx, p_ctx, p_ctx, p_ctx)


NA_QROWS = 4
NA_KROWS = NA_QROWS + NA_WIN_H


NA_NDR = 2 * NA_WIN_H


def _na_plan(rows):
    kh = min(NA_WIN_H, rows)
    masked = NA_NDR - 1
    groups = []
    for r0 in range(0, rows, NA_QROWS):
        ks = int(np.clip(r0 - kh // 2, 0, rows - NA_KROWS))
        tiles = []
        for rq in range(NA_QROWS):
            r = r0 + rq
            rs = int(np.clip(r - kh // 2, 0, rows - kh))
            slot = [ks + i - r + NA_WIN_H - 1 if rs <= ks + i < rs + kh else masked for i in range(NA_KROWS)]
            row_tiles = []
            for i in range(0, NA_KROWS, 2):
                a, b = slot[i], slot[i + 1]
                if a != masked and b != masked:
                    row_tiles.append(b)
                elif b != masked:
                    row_tiles.append(NA_NDR + b)
                else:
                    row_tiles.append(2 * NA_NDR + a)
            tiles.append(tuple(row_tiles))
        groups.append((r0, ks, tuple(tiles)))
    return tuple(groups)


def _na_bias_tiles(rpb_l):
    nh = rpb_l.shape[0]
    n_dr = 2 * NA_WIN_H - 1
    c = np.arange(GRID_W)
    cs = np.clip(c - NA_WIN_W // 2, 0, GRID_W - NA_WIN_W)
    valid = (c[None, :] >= cs[:, None]) & (c[None, :] < cs[:, None] + NA_WIN_W)
    span = 2 * GRID_W
    left = (GRID_W - 1) - (NA_WIN_W - 1)
    wv = jnp.pad(rpb_l.astype(F32), ((0, 0), (0, 0), (left, span - left - (2 * NA_WIN_W - 1))))
    skew = jnp.tile(wv, (1, 1, GRID_W))[..., :GRID_W * (span - 1)].reshape(nh, n_dr, GRID_W, span - 1)
    toe = jnp.where(valid, skew[..., GRID_W - 1:], NEG_INF)
    neg1 = jnp.full((nh, 1, GRID_W, GRID_W), NEG_INF, F32)
    negs = jnp.full((nh, NA_NDR, GRID_W, GRID_W), NEG_INF, F32)
    cur = jnp.concatenate([toe, neg1], axis=1)
    prev = jnp.concatenate([neg1, toe], axis=1)
    return jnp.concatenate([jnp.concatenate([prev, cur], -1), jnp.concatenate([negs, cur], -1),
                            jnp.concatenate([cur, negs], -1)], axis=1)


def _na_kernel(q_ref, k_ref, v_ref, z_ref, kc_ref, vc_ref, bias_ref, o_ref, kb_scr, vb_scr, *, groups):
    scale = NA_HEAD_DIM ** -0.5
    kb_scr[...] = k_ref[...].astype(BF16)
    vb_scr[...] = v_ref[...].astype(BF16)
    kctx = kc_ref[...].astype(BF16)
    vctx = vc_ref[...].astype(BF16)
    nq, nk = NA_QROWS * GRID_W, NA_KROWS * GRID_W
    for r0, ks, tiles in groups:
        qsl = slice(r0 * GRID_W, r0 * GRID_W + nq)
        ksl = slice(ks * GRID_W, ks * GRID_W + nk)
        q = q_ref[qsl, :].astype(BF16)
        bias = jnp.concatenate([jnp.concatenate([bias_ref[t] for t in row], axis=1) for row in tiles], axis=0)
        s_win = _dot_nt(q, kb_scr[ksl, :]) * scale + bias
        s_ctx = _dot_nt(q, kctx) * scale
        m = jnp.maximum(jnp.max(s_win, -1, keepdims=True), jnp.max(s_ctx, -1, keepdims=True))
        e_win = jnp.exp(s_win - m)
        e_ctx = jnp.exp(s_ctx - m)
        den = jnp.sum(e_win, -1, keepdims=True) + jnp.sum(e_ctx, -1, keepdims=True)
        o = _dot((e_win / den).astype(BF16), vb_scr[ksl, :]) + _dot((e_ctx / den).astype(BF16), vctx)
        o_ref[qsl, :] = (o * _silu(z_ref[qsl, :])).astype(BF16)


def _neighbourhood_attention(p_lat, cache_k, cache_v, rpb_l, layer, bsz, n_tok):
    rows = n_tok // GRID_W
    assert rows % NA_QROWS == 0 and rows >= NA_KROWS and NA_KROWS % 2 == 0
    groups = _na_plan(rows)
    bias_tab = _na_bias_tiles(rpb_l)
    past = cache_k.shape[2]
    hd = NA_HEAD_DIM
    blk = lambda col: pl.BlockSpec((n_tok, hd), lambda b, h: (b, col // hd + h))
    cblk = pl.BlockSpec((None, None, past, hd), lambda b, h: (b, layer, 0, h))
    return pl.pallas_call(
        functools.partial(_na_kernel, groups=groups),
        grid=(bsz, NA_HEADS),
        in_specs=[blk(COL_AQ), blk(COL_AK), blk(COL_AV), blk(COL_AZ), cblk, cblk,
                  pl.BlockSpec((None,) + bias_tab.shape[1:], lambda b, h: (h, 0, 0, 0))],
        out_specs=pl.BlockSpec((n_tok, hd), lambda b, h: (b, h)),
        out_shape=jax.ShapeDtypeStruct((bsz * n_tok, NA_WIDTH), BF16),
        scratch_shapes=[pltpu.VMEM((n_tok, hd), BF16), pltpu.VMEM((n_tok, hd), BF16)],
        compiler_params=_cparams("parallel", "parallel"),
        name="neighbourhood_attention",
    )(p_lat, p_lat, p_lat, p_lat, cache_k, cache_v, bias_tab)


def _rope_tables(seq_len):
    t = jnp.arange(seq_len)
    half = GLA_DK // 2
    inv = ROPE_BASE ** (-jnp.arange(0, half, 2, dtype=F32) / half)

    def tab(pos):
        ang = pos.astype(F32)[:, None] * inv[None, :]
        return jnp.cos(ang), jnp.sin(ang)

    cr, sr = tab(t // GRID_W)
    cc, sc = tab(t % GRID_W)
    cos = jnp.concatenate([cr, cr, cc, cc], -1)
    sin = jnp.concatenate([-sr, sr, -sc, sc], -1)
    return cos, sin


def _rope(x, cos, sin):
    quarter = GLA_DK // 4
    lane = lax.broadcasted_iota(jnp.int32, x.shape, 1)
    partner = jnp.where((lane % (2 * quarter)) < quarter,
                        pltpu.roll(x, GLA_DK - quarter, 1), pltpu.roll(x, quarter, 1))
    return x * cos + partner * sin


def _log_sigmoid(x):
    return jnp.minimum(x, 0.0) - jnp.log1p(jnp.exp(-jnp.abs(x)))


def _chunk_scan(x, reverse):
    n = x.shape[0]
    pos = lax.broadcasted_iota(jnp.int32, x.shape, 0) % GLA_CHUNK
    step = 1
    while step < GLA_CHUNK:
        if reverse:
            x = x + jnp.where(pos < GLA_CHUNK - step, pltpu.roll(x, n - step, 0), 0.0)
        else:
            x = x + jnp.where(pos >= step, pltpu.roll(x, step, 0), 0.0)
        step *= 2
    return x


GLA_GROUP = 4


def _gla_kernel(*refs, latent, n_groups, heads, layer):
    (q_ref, k_ref, v_ref, z_ref, lr_ref, wgf_ref, wgb_ref, bg_ref, ng_ref), rest = refs[:9], refs[9:]
    if latent:
        cos_ref, sin_ref, s0_ref, o_ref, qs, ks, laf, lab, o_f, o_b, st_f, st_b = rest
    else:
        ak_ref, av_ref = rest[:2]
        o_ref, sfin_ref, ko_ref, vo_ref, qs, ks, laf, lab, o_f, o_b, st_f, st_b = rest[-12:]
        if len(rest) == 15:
            for l in range(DEPTH):
                if l != layer:
                    sfin_ref[l] = jnp.zeros(sfin_ref.shape[1:], F32)
                    ko_ref[l] = jnp.zeros(ko_ref.shape[1:], F32)
                    vo_ref[l] = jnp.zeros(vo_ref.shape[1:], F32)
            sfin_ref, ko_ref, vo_ref = sfin_ref.at[layer], ko_ref.at[layer], vo_ref.at[layer]
        ko_ref[...] = ak_ref[...]
        vo_ref[...] = av_ref[...]
    c_len = GLA_CHUNK
    g_len = GLA_GROUP * c_len
    dk, dv = GLA_DK, GLA_DV

    q = q_ref[...] * (GLA_DK ** -0.5)
    k = k_ref[...]
    if latent:
        q = _rope(q, cos_ref[...], sin_ref[...])
        k = _rope(k, cos_ref[...], sin_ref[...])
    qs[...] = q
    ks[...] = k
    lrb = lr_ref[...].astype(BF16)
    laf[...] = _log_sigmoid(_dot(lrb, wgf_ref[...]) + bg_ref[0:1, :]) * (1.0 / GLA_TAU)
    lab[...] = _log_sigmoid(_dot(lrb, wgb_ref[...]) + bg_ref[1:2, :]) * (1.0 / GLA_TAU)

    ii = lax.broadcasted_iota(jnp.int32, (c_len, c_len), 0)
    jj = lax.broadcasted_iota(jnp.int32, (c_len, c_len), 1)

    def half_group(g, reverse, hh):
        la_ref, out, st = (lab, o_b, st_b) if reverse else (laf, o_f, st_f)
        kcols, vcols, srows = slice(hh * dk, (hh + 1) * dk), slice(hh * dv, (hh + 1) * dv), slice(hh * dv, (hh + 1) * dv)
        mask = (ii <= jj) if reverse else (ii >= jj)
        edge = 0 if reverse else c_len - 1
        start = g * g_len if isinstance(g, int) else pl.multiple_of(g * g_len, g_len)
        b_all = _chunk_scan(la_ref[pl.ds(start, g_len), kcols], reverse)
        s_t = st[srows, :]
        for c in (range(GLA_GROUP - 1, -1, -1) if reverse else range(GLA_GROUP)):
            sl = pl.ds(start + c * c_len, c_len)
            b = b_all[c * c_len:(c + 1) * c_len]
            btot = b[edge:edge + 1, :]
            qc = qs[sl, kcols]
            kc = ks[sl, kcols]
            vc = v_ref[sl, vcols]
            qe = (qc * jnp.exp(b)).astype(BF16)
            ke = (kc * jnp.exp(-b)).astype(BF16)
            kw = (kc * jnp.exp(btot - b)).astype(BF16)
            att = jnp.where(mask, _dot_nt(qe, ke), 0.0)
            out[sl, vcols] = _dot(att.astype(BF16), vc.astype(BF16)) + _dot_nt(qe, s_t.astype(BF16))
            s_t = jnp.exp(btot) * s_t + _dot(vc.T.astype(BF16), kw)
        st[srows, :] = s_t

    for direction, st in enumerate((st_f, st_b)):
        st[...] = s0_ref[direction].T if latent else jnp.zeros_like(st)
    if n_groups == 1:
        for hh in range(heads):
            half_group(0, False, hh)
            half_group(0, True, hh)
    else:
        def body(i, carry):
            for hh in range(heads):
                half_group(i, False, hh)
                half_group(n_groups - 1 - i, True, hh)
            return carry
        lax.fori_loop(0, n_groups, body, 0)

    for hh in range(heads):
        kcols, vcols = slice(hh * dk, (hh + 1) * dk), slice(hh * dv, (hh + 1) * dv)
        if not latent:
            sfin_ref[0, hh] = st_f[vcols, :].T
            sfin_ref[1, hh] = st_b[vcols, :].T
        o = (o_f[:, vcols] + o_b[:, vcols]) - jnp.sum(qs[:, kcols] * ks[:, kcols], -1, keepdims=True) * v_ref[:, vcols]
        o = o * lax.rsqrt(jnp.mean(o * o, -1, keepdims=True) + LN_EPS) * ng_ref[:, vcols]
        o_ref[:, vcols] = (o * _silu(z_ref[:, vcols])).astype(BF16)


def _gla(p, p_lr, wg_f, wg_b, b_gate, norm_g, bsz, seq_len, layer, rope=None, state=None, state_prev=(), after=None):
    latent = state is not None
    assert seq_len % (GLA_GROUP * GLA_CHUNK) == 0
    dk, dv = GLA_DK, GLA_DV
    hps = 1 if latent else GLA_HEADS
    kw, vw = hps * dk, hps * dv
    in_specs = [pl.BlockSpec((seq_len, kw), lambda b, h: (b, COL_BQ // kw + h)),
                pl.BlockSpec((seq_len, kw), lambda b, h: (b, COL_BK // kw + h)),
                pl.BlockSpec((seq_len, vw), lambda b, h: (b, COL_BV // vw + h)),
                pl.BlockSpec((seq_len, vw), lambda b, h: (b, COL_BZ // vw + h)),
                pl.BlockSpec((seq_len, LANES), lambda b, h: (b, 0)),
                pl.BlockSpec((LANES, kw), lambda b, h: (0, h)),
                pl.BlockSpec((LANES, kw), lambda b, h: (0, h)),
                pl.BlockSpec((2, kw), lambda b, h: (0, h)),
                pl.BlockSpec((1, vw), lambda b, h: (0, h))]
    args = [p, p, p, p, p_lr, wg_f, wg_b, b_gate, norm_g]
    o_spec = pl.BlockSpec((seq_len, vw), lambda b, h: (b, h))
    o_shape = jax.ShapeDtypeStruct((bsz * seq_len, GLA_VW), BF16)
    if latent:
        cos, sin = rope
        in_specs += [pl.BlockSpec((seq_len, dk), lambda b, h: (0, 0)),
                     pl.BlockSpec((seq_len, dk), lambda b, h: (0, 0)),
                     pl.BlockSpec((None, None, 2, None, dk, dv), lambda b, h: (b, layer, 0, h, 0, 0))]
        args += [cos, sin, state]
        out_specs, out_shape, aliases = o_spec, o_shape, {}
    else:
        assert hps == GLA_HEADS
        in_specs += [pl.BlockSpec((seq_len, NA_WIDTH), lambda b, h: (b, COL_AK // NA_WIDTH)),
                     pl.BlockSpec((seq_len, NA_WIDTH), lambda b, h: (b, COL_AV // NA_WIDTH)),
                     pl.BlockSpec(memory_space=pl.ANY)]
        args += [p, p, after]
        aliases = {len(args) + i: 1 + i for i in range(len(state_prev))}
        in_specs += [pl.BlockSpec(memory_space=pl.ANY)] * len(state_prev)
        args += list(state_prev)
        if state_prev:
            s_spec = pl.BlockSpec((None, None, 2, hps, dk, dv), lambda b, h: (b, layer, 0, h, 0, 0))
            kv_spec = pl.BlockSpec((None, None, seq_len, NA_WIDTH), lambda b, h: (b, layer, 0, 0))
        else:
            s_spec = pl.BlockSpec((None, DEPTH, 2, hps, dk, dv), lambda b, h: (b, 0, 0, h, 0, 0))
            kv_spec = pl.BlockSpec((None, DEPTH, seq_len, NA_WIDTH), lambda b, h: (b, 0, 0, 0))
        kv_shape = jax.ShapeDtypeStruct((bsz, DEPTH, seq_len, NA_WIDTH), F32)
        out_specs = [o_spec, s_spec, kv_spec, kv_spec]
        out_shape = [o_shape, jax.ShapeDtypeStruct((bsz, DEPTH, 2, GLA_HEADS, dk, dv), F32), kv_shape, kv_shape]
    return pl.pallas_call(
        functools.partial(_gla_kernel, latent=latent, n_groups=seq_len // (GLA_GROUP * GLA_CHUNK), heads=hps,
                          layer=layer),
        grid=(bsz, GLA_HEADS // hps),
        in_specs=in_specs,
        out_specs=out_specs,
        out_shape=out_shape,
        input_output_aliases=aliases,
        scratch_shapes=[pltpu.VMEM((seq_len, kw), F32), pltpu.VMEM((seq_len, kw), F32),
                        pltpu.VMEM((seq_len, kw), F32), pltpu.VMEM((seq_len, kw), F32),
                        pltpu.VMEM((seq_len, vw), F32), pltpu.VMEM((seq_len, vw), F32),
                        pltpu.VMEM((vw, dk), F32), pltpu.VMEM((vw, dk), F32)],
        compiler_params=_cparams("parallel", "parallel"),
        name="gla_latent" if latent else "gla_context",
    )(*args)


def _conv_merge_kernel(u_ref, g_ref, up_ref, gp_ref, un_ref, gn_ref, z_ref, w_ref, b_ref, ng_ref, nb_ref,
                       oa_ref, ob_ref, ga_ref, gb_ref, gc_ref, wa_ref, wb_ref, wc_ref, o_ref, pad, cv,
                       *, tiles_per_seq, tt):
    m = jax.nn.sigmoid(ga_ref[...]) * _dot(oa_ref[...], wa_ref[...])
    m = m + jax.nn.sigmoid(gb_ref[...]) * _dot(ob_ref[...], wb_ref[...])
    t = pl.program_id(0) % tiles_per_seq
    halo = CONV_HALO
    glu = lambda u, g: u * jax.nn.sigmoid(g)
    pad[halo:halo + tt, :] = glu(u_ref[...], g_ref[...])
    pad[0:halo, :] = jnp.where(t > 0, glu(up_ref[...], gp_ref[...]), 0.0)
    pad[halo + tt:halo + tt + halo, :] = jnp.where(t < tiles_per_seq - 1, glu(un_ref[...], gn_ref[...]), 0.0)
    first = halo - CONV_TAPS // 2
    rb = 64
    reach = -(-(first + CONV_TAPS - 1) // SUBLANES) * SUBLANES
    for c0 in range(0, CONV_WIDTH, LANES):
        lanes = slice(c0, c0 + LANES)
        w_strip = w_ref[:, lanes]
        for r0 in range(0, tt, rb):
            base = pad[r0:r0 + rb + reach, lanes]
            acc = jnp.broadcast_to(b_ref[:, lanes], (rb, LANES))
            for phase in range(SUBLANES):
                rows = pltpu.roll(base, rb + reach - phase, 0) if phase else base
                for a in range(0, reach, SUBLANES):
                    j = a + phase - first
                    if 0 <= j < CONV_TAPS:
                        acc = acc + rows[a:a + rb] * w_strip[j:j + 1, :]
            cv[r0:r0 + rb, lanes] = acc
    acc = cv[...]
    mu = jnp.mean(acc, -1, keepdims=True)
    xc = acc - mu
    var = jnp.mean(xc * xc, -1, keepdims=True)
    y = xc * lax.rsqrt(var + LN_EPS) * ng_ref[...] + nb_ref[...]
    oc = (_silu(y) * _silu(z_ref[...])).astype(BF16)
    m = m + jax.nn.sigmoid(gc_ref[...]) * _dot(oc, wc_ref[...])
    o_ref[...] = m.astype(BF16)


def _conv_merge(p, og_a, og_b, conv_w, conv_b, norm_g, norm_b, wa, wb, wc, seq_len):
    t_tok = p.shape[0]
    tt = 256
    tiles_per_seq = seq_len // tt
    w = CONV_WIDTH
    d = D_MODEL
    hb = tt // CONV_HALO
    n_halo = t_tok // CONV_HALO
    main = lambda col: pl.BlockSpec((tt, w), lambda i: (i, col // w))
    prev = lambda col: pl.BlockSpec((CONV_HALO, w), lambda i: (jnp.maximum(i * hb - 1, 0), col // w))
    nxt = lambda col: pl.BlockSpec((CONV_HALO, w), lambda i: (jnp.minimum((i + 1) * hb, n_halo - 1), col // w))
    vec = pl.BlockSpec((1, w), lambda i: (0, 0))
    br = pl.BlockSpec((tt, w), lambda i: (i, 0))
    gate = lambda col: pl.BlockSpec((tt, d), lambda i: (i, col // d))
    wspec = pl.BlockSpec((w, d), lambda i: (0, 0), pipeline_mode=pl.Buffered(1))
    return pl.pallas_call(
        functools.partial(_conv_merge_kernel, tiles_per_seq=tiles_per_seq, tt=tt),
        grid=(t_tok // tt,),
        in_specs=[main(COL_CU), main(COL_CG), prev(COL_CU), prev(COL_CG), nxt(COL_CU), nxt(COL_CG), main(COL_CZ),
                  pl.BlockSpec((CONV_TAPS, w), lambda i: (0, 0)), vec, vec, vec,
                  br, br, gate(COL_GA), gate(COL_GB), gate(COL_GC), wspec, wspec, wspec],
        out_specs=pl.BlockSpec((tt, d), lambda i: (i, 0)),
        out_shape=jax.ShapeDtypeStruct((t_tok, d), BF16),
        scratch_shapes=[pltpu.VMEM((tt + 2 * CONV_HALO, w), F32), pltpu.VMEM((tt, w), F32)],
        compiler_params=_cparams("parallel"),
        name="conv_merge",
    )(p, p, p, p, p, p, p, conv_w, conv_b, norm_g, norm_b, og_a, og_b, p, p, p, wa, wb, wc)


OUT_SUB = 256


def _out_kernel(m_ref, w_ref, x_ref, mod_ref, g_ref, b_ref, o_ref):
    gate = mod_ref[0, :, 2 * D_MODEL:3 * D_MODEL]
    for r0 in range(0, m_ref.shape[0], OUT_SUB):
        rows = slice(r0, r0 + OUT_SUB)
        y = ALPHA * x_ref[rows, :] + gate * _dot(m_ref[rows, :], w_ref[...])
        mu = jnp.mean(y, -1, keepdims=True)
        yc = y - mu
        var = jnp.mean(yc * yc, -1, keepdims=True)
        o_ref[rows, :] = yc * lax.rsqrt(var + LN_EPS) * g_ref[...] + b_ref[...]


def _out_projection(merged, w_out, x2d, mod, ln_g, ln_b, seq_len):
    t_tok = x2d.shape[0]
    tm = 2 * OUT_SUB
    d = D_MODEL
    per = seq_len // tm if mod.shape[0] > 1 else t_tok // tm
    row = pl.BlockSpec((tm, d), lambda i: (i, 0))
    vec = pl.BlockSpec((1, d), lambda i: (0, 0))
    return pl.pallas_call(
        _out_kernel,
        grid=(t_tok // tm,),
        in_specs=[row, pl.BlockSpec((d, d), lambda i: (0, 0), pipeline_mode=pl.Buffered(1)), row,
                  pl.BlockSpec((1, 1, 3 * d), lambda i: (i // per, 0, 0)), vec, vec],
        out_specs=row,
        out_shape=jax.ShapeDtypeStruct((t_tok, d), F32),
        compiler_params=_cparams("parallel"),
        name="out_projection",
    )(merged, w_out, x2d, mod, ln_g, ln_b)


def kernel(x_prompt, x_sample, cache_k, cache_v, state_gla, c, c_ctx, w_mod, b_mod, w_in, rpb, gla_w_gate,
           gla_b_gate, gla_norm_g, conv_w, conv_b, conv_norm_g, conv_norm_b, w_proj_a, w_proj_b, w_proj_c,
           w_out, ln_g, ln_b):
    bsz, seq_len, d = x_prompt.shape
    dbsz, dseq, _ = x_sample.shape

    cvec = jnp.zeros((8, d), F32).at[0].set(c_ctx).at[1:1 + dbsz].set(c)
    mod_all = _modulation(cvec, w_mod, b_mod)

    w_in_t = jnp.swapaxes(w_in, 1, 2)
    w_t = _cast_rows_bf16(w_in_t)
    w_lr_t = jnp.pad(w_in_t[:, LR_START:LR_START + 2 * GLA_LOWRANK],
                     ((0, 0), (0, LANES - 2 * GLA_LOWRANK), (0, 0))).astype(BF16)
    wg = gla_w_gate.astype(BF16)
    wg_f = jnp.pad(wg[:, 0], ((0, 0), (0, LANES - GLA_LOWRANK), (0, 0)))
    wg_b = jnp.pad(wg[:, 1], ((0, 0), (GLA_LOWRANK, LANES - 2 * GLA_LOWRANK), (0, 0)))
    wpa, wpb, wpc, wo = (w.astype(BF16) for w in (w_proj_a, w_proj_b, w_proj_c, w_out))
    ck, cv = (a.reshape(a.shape[:3] + (NA_WIDTH,)) for a in (cache_k, cache_v))
    rope = _rope_tables(dseq)

    h_ctx = x_prompt.reshape(bsz * seq_len, d)
    h_lat = x_sample.reshape(dbsz * dseq, d)
    stacked = ()
    for l in range(DEPTH):
        mod_ctx = mod_all[l, 0:1][:, None, :]
        mod_lat = mod_all[l, 1:1 + dbsz][:, None, :]
        row = lambda a: a[l][None, :]
        gla_w = (wg_f[l], wg_b[l], gla_b_gate[l], row(gla_norm_g))
        conv_p = (conv_w[l], row(conv_b), row(conv_norm_g), row(conv_norm_b))

        def tail(x2d, p, og_a, og_b, mod, seq):
            merged = _conv_merge(p, og_a, og_b, *conv_p, wpa[l], wpb[l], wpc[l], seq)
            return _out_projection(merged, wo[l], x2d, mod, row(ln_g), row(ln_b), seq)

        p, p_lr = _in_projection(h_ctx, mod_ctx, w_t, w_lr_t, l, seq_len)
        og_a = _context_attention(p, bsz, seq_len)
        og_b, *stacked = _gla(p, p_lr, *gla_w, bsz, seq_len, l, state_prev=stacked, after=og_a)
        h_ctx = tail(h_ctx, p, og_a, og_b, mod_ctx, seq_len)

        p, p_lr = _in_projection(h_lat, mod_lat, w_t, w_lr_t, l, dseq)
        og_a = _neighbourhood_attention(p, ck, cv, rpb[l], l, dbsz, dseq)
        og_b = _gla(p, p_lr, *gla_w, dbsz, dseq, l, rope=rope, state=state_gla)
        h_lat = tail(h_lat, p, og_a, og_b, mod_lat, dseq)

    new_state, new_k, new_v = stacked
    new_k, new_v = (a.reshape(bsz, DEPTH, seq_len, NA_HEADS, NA_HEAD_DIM) for a in (new_k, new_v))
    return (h_ctx.reshape(bsz, seq_len, d), h_lat.reshape(dbsz, dseq, d), new_k, new_v, new_state)
```

```python
import functools

import numpy as np
import jax
import jax.numpy as jnp
from jax import lax
from jax.experimental import pallas as pl
from jax.experimental.pallas import tpu as pltpu

D_MODEL = 2048
DEPTH = 2
GRID_W = 64
NA_HEADS = 8
NA_HEAD_DIM = 128
NA_WIDTH = NA_HEADS * NA_HEAD_DIM
NA_WIN_H = 8
NA_WIN_W = 16
GLA_HEADS = 4
GLA_DK = 128
GLA_DV = 256
GLA_KW = GLA_HEADS * GLA_DK
GLA_VW = GLA_HEADS * GLA_DV
GLA_LOWRANK = 16
GLA_TAU = 16.0
GLA_CHUNK = 64
ROPE_BASE = 10000.0
CONV_WIDTH = 1024
CONV_TAPS = 31
ALPHA = (2 * DEPTH) ** 0.25
LN_EPS = 1e-5
NEG_INF = -1e30

LANES = 128
SUBLANES = 8
BF16_ROWS = 2 * SUBLANES
CONV_HALO = 16
VMEM_LIMIT = 60 * 1024 * 1024

COL_AQ, COL_AK, COL_AV, COL_AZ = 0, 1024, 2048, 3072
COL_BQ, COL_BK, COL_BV, COL_BZ = 4096, 4608, 5120, 6144
COL_CU, COL_CG, COL_CZ = 7168, 8192, 9216
COL_GA, COL_GB, COL_GC = 10240, 12288, 14336
N_MAIN = 16384
LR_START = 6144

BF16 = jnp.bfloat16
F32 = jnp.float32


def _cparams(*sem):
    return pltpu.CompilerParams(dimension_semantics=sem, vmem_limit_bytes=VMEM_LIMIT)


def _silu(x):
    return x * jax.nn.sigmoid(x)


def _dot(a, b):
    return jnp.dot(a, b, preferred_element_type=F32)


def _dot_nt(a, b):
    return lax.dot_general(a, b, (((1,), (1,)), ((), ())), preferred_element_type=F32)


def _mod_kernel(cv_ref, w_ref, b_ref, o_ref):
    s = _silu(cv_ref[...]).astype(BF16)
    o_ref[0] = _dot(s, w_ref[0].astype(BF16)) + b_ref[0]


def _modulation(cvec, w_mod, b_mod):
    tn = 1024
    n3 = 3 * D_MODEL
    return pl.pallas_call(
        _mod_kernel,
        grid=(DEPTH, n3 // tn),
        in_specs=[pl.BlockSpec((8, D_MODEL), lambda l, n: (0, 0)),
                  pl.BlockSpec((1, D_MODEL, tn), lambda l, n: (l, 0, n)),
                  pl.BlockSpec((1, 1, tn), lambda l, n: (l, 0, n))],
        out_specs=pl.BlockSpec((1, 8, tn), lambda l, n: (l, 0, n)),
        out_shape=jax.ShapeDtypeStruct((DEPTH, 8, n3), F32),
        compiler_params=_cparams("parallel", "parallel"),
        name="modulation",
    )(cvec, w_mod, b_mod.reshape(DEPTH, 1, n3))


def _cast_kernel(a_ref, o_ref):
    o_ref[...] = a_ref[...].astype(BF16)


def _cast_rows_bf16(w_t):
    depth, rows, d = w_t.shape
    tr = max(t for t in range(BF16_ROWS, 1024 + 1, BF16_ROWS) if rows % t == 0)
    spec = pl.BlockSpec((1, tr, d), lambda l, r: (l, r, 0))
    return pl.pallas_call(
        _cast_kernel,
        grid=(depth, rows // tr),
        in_specs=[spec],
        out_specs=spec,
        out_shape=jax.ShapeDtypeStruct(w_t.shape, BF16),
        compiler_params=_cparams("parallel", "parallel"),
        name="cast_w_in",
    )(w_t)


def _inproj_kernel(x_ref, mod_ref, w_ref, wlr_ref, o_ref, olr_ref, h_scr):
    strip = 256

    @pl.when(pl.program_id(1) == 0)
    def _():
        shift = mod_ref[0, :, 0:D_MODEL]
        scale = mod_ref[0, :, D_MODEL:2 * D_MODEL]

        def body(i, carry):
            sl = pl.ds(pl.multiple_of(i * strip, strip), strip)
            x = x_ref[sl, :]
            mu = jnp.mean(x, -1, keepdims=True)
            xc = x - mu
            var = jnp.mean(xc * xc, -1, keepdims=True)
            y = xc * lax.rsqrt(var + LN_EPS)
            hb = (y * (1.0 + scale) + shift).astype(BF16)
            h_scr[sl, :] = hb
            olr_ref[sl, :] = _dot_nt(hb, wlr_ref[...])
            return carry

        lax.fori_loop(0, x_ref.shape[0] // strip, body, 0)

    o_ref[...] = _dot_nt(h_scr[...], w_ref[0])


def _in_projection(x2d, mod, w_t, w_lr_t, layer, seq_len):
    t_tok = x2d.shape[0]
    tm, tn = 1024, 2048
    assert LR_START % tn == 0
    per = seq_len // tm if mod.shape[0] > 1 else t_tok // tm
    w_row = lambda n: pl.multiple_of(n * tn + jnp.where(n >= LR_START // tn, 2 * GLA_LOWRANK, 0), 2 * GLA_LOWRANK)
    return pl.pallas_call(
        _inproj_kernel,
        grid=(t_tok // tm, N_MAIN // tn),
        in_specs=[pl.BlockSpec((tm, D_MODEL), lambda m, n: (m, 0)),
                  pl.BlockSpec((1, 1, 3 * D_MODEL), lambda m, n: (m // per, 0, 0)),
                  pl.BlockSpec((pl.Element(1), pl.Element(tn), pl.Element(D_MODEL)),
                               lambda m, n: (layer, w_row(n), 0)),
                  pl.BlockSpec((None, LANES, D_MODEL), lambda m, n: (layer, 0, 0))],
        out_specs=[pl.BlockSpec((tm, tn), lambda m, n: (m, n)),
                   pl.BlockSpec((tm, LANES), lambda m, n: (m, 0))],
        out_shape=[jax.ShapeDtypeStruct((t_tok, N_MAIN), F32),
                   jax.ShapeDtypeStruct((t_tok, LANES), F32)],
        scratch_shapes=[pltpu.VMEM((tm, D_MODEL), BF16)],
        compiler_params=_cparams("parallel", "arbitrary"),
        name="in_projection",
    )(x2d, mod, w_t, w_lr_t)


def _ctx_attn_kernel(q_ref, k_ref, v_ref, z_ref, o_ref, *, seq_len):
    scale = NA_HEAD_DIM ** -0.5
    for r0 in range(0, q_ref.shape[0], seq_len):
        rows = slice(r0, r0 + seq_len)
        for h in range(NA_HEADS):
            cs = slice(h * NA_HEAD_DIM, (h + 1) * NA_HEAD_DIM)
            q = q_ref[rows, cs].astype(BF16)
            k = k_ref[rows, cs].astype(BF16)
            v = v_ref[rows, cs].astype(BF16)
            s = _dot_nt(q, k) * scale
            e = jnp.exp(s - jnp.max(s, -1, keepdims=True))
            p = e / jnp.sum(e, -1, keepdims=True)
            o = _dot(p.astype(BF16), v)
            o_ref[rows, cs] = (o * _silu(z_ref[rows, cs])).astype(BF16)


def _context_attention(p_ctx, bsz, seq_len):
    per_step = 2 if bsz % 2 == 0 else 1
    tr = per_step * seq_len
    blk = lambda j: pl.BlockSpec((tr, NA_WIDTH), lambda b: (b, j))
    return pl.pallas_call(
        functools.partial(_ctx_attn_kernel, seq_len=seq_len),
        grid=(bsz // per_step,),
        in_specs=[blk(COL_AQ // NA_WIDTH), blk(COL_AK // NA_WIDTH), blk(COL_AV // NA_WIDTH), blk(COL_AZ // NA_WIDTH)],
        out_specs=pl.BlockSpec((tr, NA_WIDTH), lambda b: (b, 0)),
        out_shape=jax.ShapeDtypeStruct((bsz * seq_len, NA_WIDTH), BF16),
        compiler_params=_cparams("parallel"),
        name="context_attention",
    )(p_ctx, p_ctx, p_ctx, p_ctx)


NA_QROWS = 4
NA_KROWS = NA_QROWS + NA_WIN_H


NA_NDR = 2 * NA_WIN_H


def _na_plan(rows):
    kh = min(NA_WIN_H, rows)
    masked = NA_NDR - 1
    groups = []
    for r0 in range(0, rows, NA_QROWS):
        ks = int(np.clip(r0 - kh // 2, 0, rows - NA_KROWS))
        tiles = []
        for rq in range(NA_QROWS):
            r = r0 + rq
            rs = int(np.clip(r - kh // 2, 0, rows - kh))
            slot = [ks + i - r + NA_WIN_H - 1 if rs <= ks + i < rs + kh else masked for i in range(NA_KROWS)]
            row_tiles = []
            for i in range(0, NA_KROWS, 2):
                a, b = slot[i], slot[i + 1]
                if a != masked and b != masked:
                    row_tiles.append(b)
                elif b != masked:
                    row_tiles.append(NA_NDR + b)
                else:
                    row_tiles.append(2 * NA_NDR + a)
            tiles.append(tuple(row_tiles))
        groups.append((r0, ks, tuple(tiles)))
    return tuple(groups)


def _na_bias_tiles(rpb_l):
    nh = rpb_l.shape[0]
    n_dr = 2 * NA_WIN_H - 1
    c = np.arange(GRID_W)
    cs = np.clip(c - NA_WIN_W // 2, 0, GRID_W - NA_WIN_W)
    valid = (c[None, :] >= cs[:, None]) & (c[None, :] < cs[:, None] + NA_WIN_W)
    span = 2 * GRID_W
    left = (GRID_W - 1) - (NA_WIN_W - 1)
    wv = jnp.pad(rpb_l.astype(F32), ((0, 0), (0, 0), (left, span - left - (2 * NA_WIN_W - 1))))
    skew = jnp.tile(wv, (1, 1, GRID_W))[..., :GRID_W * (span - 1)].reshape(nh, n_dr, GRID_W, span - 1)
    toe = jnp.where(valid, skew[..., GRID_W - 1:], NEG_INF)
    neg1 = jnp.full((nh, 1, GRID_W, GRID_W), NEG_INF, F32)
    negs = jnp.full((nh, NA_NDR, GRID_W, GRID_W), NEG_INF, F32)
    cur = jnp.concatenate([toe, neg1], axis=1)
    prev = jnp.concatenate([neg1, toe], axis=1)
    return jnp.concatenate([jnp.concatenate([prev, cur], -1), jnp.concatenate([negs, cur], -1),
                            jnp.concatenate([cur, negs], -1)], axis=1)


def _na_kernel(q_ref, k_ref, v_ref, z_ref, kc_ref, vc_ref, bias_ref, o_ref, kb_scr, vb_scr, *, groups):
    scale = NA_HEAD_DIM ** -0.5
    kb_scr[...] = k_ref[...].astype(BF16)
    vb_scr[...] = v_ref[...].astype(BF16)
    kctx = kc_ref[...].astype(BF16)
    vctx = vc_ref[...].astype(BF16)
    nq, nk = NA_QROWS * GRID_W, NA_KROWS * GRID_W
    for r0, ks, tiles in groups:
        qsl = slice(r0 * GRID_W, r0 * GRID_W + nq)
        ksl = slice(ks * GRID_W, ks * GRID_W + nk)
        q = q_ref[qsl, :].astype(BF16)
        bias = jnp.concatenate([jnp.concatenate([bias_ref[t] for t in row], axis=1) for row in tiles], axis=0)
        s_win = _dot_nt(q, kb_scr[ksl, :]) * scale + bias
        s_ctx = _dot_nt(q, kctx) * scale
        m = jnp.maximum(jnp.max(s_win, -1, keepdims=True), jnp.max(s_ctx, -1, keepdims=True))
        e_win = jnp.exp(s_win - m)
        e_ctx = jnp.exp(s_ctx - m)
        den = jnp.sum(e_win, -1, keepdims=True) + jnp.sum(e_ctx, -1, keepdims=True)
        o = _dot((e_win / den).astype(BF16), vb_scr[ksl, :]) + _dot((e_ctx / den).astype(BF16), vctx)
        o_ref[qsl, :] = (o * _silu(z_ref[qsl, :])).astype(BF16)


def _neighbourhood_attention(p_lat, cache_k, cache_v, rpb_l, layer, bsz, n_tok):
    rows = n_tok // GRID_W
    assert rows % NA_QROWS == 0 and rows >= NA_KROWS and NA_KROWS % 2 == 0
    groups = _na_plan(rows)
    bias_tab = _na_bias_tiles(rpb_l)
    past = cache_k.shape[2]
    hd = NA_HEAD_DIM
    blk = lambda col: pl.BlockSpec((n_tok, hd), lambda b, h: (b, col // hd + h))
    cblk = pl.BlockSpec((None, None, past, hd), lambda b, h: (b, layer, 0, h))
    return pl.pallas_call(
        functools.partial(_na_kernel, groups=groups),
        grid=(bsz, NA_HEADS),
        in_specs=[blk(COL_AQ), blk(COL_AK), blk(COL_AV), blk(COL_AZ), cblk, cblk,
                  pl.BlockSpec((None,) + bias_tab.shape[1:], lambda b, h: (h, 0, 0, 0))],
        out_specs=pl.BlockSpec((n_tok, hd), lambda b, h: (b, h)),
        out_shape=jax.ShapeDtypeStruct((bsz * n_tok, NA_WIDTH), BF16),
        scratch_shapes=[pltpu.VMEM((n_tok, hd), BF16), pltpu.VMEM((n_tok, hd), BF16)],
        compiler_params=_cparams("parallel", "parallel"),
        name="neighbourhood_attention",
    )(p_lat, p_lat, p_lat, p_lat, cache_k, cache_v, bias_tab)


def _rope_tables(seq_len):
    t = jnp.arange(seq_len)
    half = GLA_DK // 2
    inv = ROPE_BASE ** (-jnp.arange(0, half, 2, dtype=F32) / half)

    def tab(pos):
        ang = pos.astype(F32)[:, None] * inv[None, :]
        return jnp.cos(ang), jnp.sin(ang)

    cr, sr = tab(t // GRID_W)
    cc, sc = tab(t % GRID_W)
    cos = jnp.concatenate([cr, cr, cc, cc], -1)
    sin = jnp.concatenate([-sr, sr, -sc, sc], -1)
    return cos, sin


def _rope(x, cos, sin):
    quarter = GLA_DK // 4
    lane = lax.broadcasted_iota(jnp.int32, x.shape, 1)
    partner = jnp.where((lane % (2 * quarter)) < quarter,
                        pltpu.roll(x, GLA_DK - quarter, 1), pltpu.roll(x, quarter, 1))
    return x * cos + partner * sin


def _log_sigmoid(x):
    return jnp.minimum(x, 0.0) - jnp.log1p(jnp.exp(-jnp.abs(x)))


def _chunk_scan(x, reverse):
    n = x.shape[0]
    pos = lax.broadcasted_iota(jnp.int32, x.shape, 0) % GLA_CHUNK
    step = 1
    while step < GLA_CHUNK:
        if reverse:
            x = x + jnp.where(pos < GLA_CHUNK - step, pltpu.roll(x, n - step, 0), 0.0)
        else:
            x = x + jnp.where(pos >= step, pltpu.roll(x, step, 0), 0.0)
        step *= 2
    return x


GLA_GROUP = 4


def _gla_kernel(*refs, latent, n_groups, heads, layer):
    (q_ref, k_ref, v_ref, z_ref, lr_ref, wgf_ref, wgb_ref, bg_ref, ng_ref), rest = refs[:9], refs[9:]
    if latent:
        cos_ref, sin_ref, s0_ref, o_ref, qs, ks, laf, lab, o_f, o_b, st_f, st_b = rest
    else:
        ak_ref, av_ref = rest[:2]
        o_ref, sfin_ref, ko_ref, vo_ref, qs, ks, laf, lab, o_f, o_b, st_f, st_b = rest[-12:]
        if len(rest) == 15:
            for l in range(DEPTH):
                if l != layer:
                    sfin_ref[l] = jnp.zeros(sfin_ref.shape[1:], F32)
                    ko_ref[l] = jnp.zeros(ko_ref.shape[1:], F32)
                    vo_ref[l] = jnp.zeros(vo_ref.shape[1:], F32)
            sfin_ref, ko_ref, vo_ref = sfin_ref.at[layer], ko_ref.at[layer], vo_ref.at[layer]
        ko_ref[...] = ak_ref[...]
        vo_ref[...] = av_ref[...]
    c_len = GLA_CHUNK
    g_len = GLA_GROUP * c_len
    dk, dv = GLA_DK, GLA_DV

    q = q_ref[...] * (GLA_DK ** -0.5)
    k = k_ref[...]
    if latent:
        q = _rope(q, cos_ref[...], sin_ref[...])
        k = _rope(k, cos_ref[...], sin_ref[...])
    qs[...] = q
    ks[...] = k
    lrb = lr_ref[...].astype(BF16)
    laf[...] = _log_sigmoid(_dot(lrb, wgf_ref[...]) + bg_ref[0:1, :]) * (1.0 / GLA_TAU)
    lab[...] = _log_sigmoid(_dot(lrb, wgb_ref[...]) + bg_ref[1:2, :]) * (1.0 / GLA_TAU)

    ii = lax.broadcasted_iota(jnp.int32, (c_len, c_len), 0)
    jj = lax.broadcasted_iota(jnp.int32, (c_len, c_len), 1)

    def half_group(g, reverse, hh):
        la_ref, out, st = (lab, o_b, st_b) if reverse else (laf, o_f, st_f)
        kcols, vcols, srows = slice(hh * dk, (hh + 1) * dk), slice(hh * dv, (hh + 1) * dv), slice(hh * dv, (hh + 1) * dv)
        mask = (ii <= jj) if reverse else (ii >= jj)
        edge = 0 if reverse else c_len - 1
        start = g * g_len if isinstance(g, int) else pl.multiple_of(g * g_len, g_len)
        b_all = _chunk_scan(la_ref[pl.ds(start, g_len), kcols], reverse)
        s_t = st[srows, :]
        for c in (range(GLA_GROUP - 1, -1, -1) if reverse else range(GLA_GROUP)):
            sl = pl.ds(start + c * c_len, c_len)
            b = b_all[c * c_len:(c + 1) * c_len]
            btot = b[edge:edge + 1, :]
            qc = qs[sl, kcols]
            kc = ks[sl, kcols]
            vc = v_ref[sl, vcols]
            qe = (qc * jnp.exp(b)).astype(BF16)
            ke = (kc * jnp.exp(-b)).astype(BF16)
            kw = (kc * jnp.exp(btot - b)).astype(BF16)
            att = jnp.where(mask, _dot_nt(qe, ke), 0.0)
            out[sl, vcols] = _dot(att.astype(BF16), vc.astype(BF16)) + _dot_nt(qe, s_t.astype(BF16))
            s_t = jnp.exp(btot) * s_t + _dot(vc.T.astype(BF16), kw)
        st[srows, :] = s_t

    for direction, st in enumerate((st_f, st_b)):
        st[...] = s0_ref[direction].T if latent else jnp.zeros_like(st)
    if n_groups == 1:
        for hh in range(heads):
            half_group(0, False, hh)
            half_group(0, True, hh)
    else:
        def body(i, carry):
            for hh in range(heads):
                half_group(i, False, hh)
                half_group(n_groups - 1 - i, True, hh)
            return carry
        lax.fori_loop(0, n_groups, body, 0)

    for hh in range(heads):
        kcols, vcols = slice(hh * dk, (hh + 1) * dk), slice(hh * dv, (hh + 1) * dv)
        if not latent:
            sfin_ref[0, hh] = st_f[vcols, :].T
            sfin_ref[1, hh] = st_b[vcols, :].T
        o = (o_f[:, vcols] + o_b[:, vcols]) - jnp.sum(qs[:, kcols] * ks[:, kcols], -1, keepdims=True) * v_ref[:, vcols]
        o = o * lax.rsqrt(jnp.mean(o * o, -1, keepdims=True) + LN_EPS) * ng_ref[:, vcols]
        o_ref[:, vcols] = (o * _silu(z_ref[:, vcols])).astype(BF16)


def _gla(p, p_lr, wg_f, wg_b, b_gate, norm_g, bsz, seq_len, layer, rope=None, state=None, state_prev=(), after=None):
    latent = state is not None
    assert seq_len % (GLA_GROUP * GLA_CHUNK) == 0
    dk, dv = GLA_DK, GLA_DV
    hps = 1 if latent else GLA_HEADS
    kw, vw = hps * dk, hps * dv
    in_specs = [pl.BlockSpec((seq_len, kw), lambda b, h: (b, COL_BQ // kw + h)),
                pl.BlockSpec((seq_len, kw), lambda b, h: (b, COL_BK // kw + h)),
                pl.BlockSpec((seq_len, vw), lambda b, h: (b, COL_BV // vw + h)),
                pl.BlockSpec((seq_len, vw), lambda b, h: (b, COL_BZ // vw + h)),
                pl.BlockSpec((seq_len, LANES), lambda b, h: (b, 0)),
                pl.BlockSpec((LANES, kw), lambda b, h: (0, h)),
                pl.BlockSpec((LANES, kw), lambda b, h: (0, h)),
                pl.BlockSpec((2, kw), lambda b, h: (0, h)),
                pl.BlockSpec((1, vw), lambda b, h: (0, h))]
    args = [p, p, p, p, p_lr, wg_f, wg_b, b_gate, norm_g]
    o_spec = pl.BlockSpec((seq_len, vw), lambda b, h: (b, h))
    o_shape = jax.ShapeDtypeStruct((bsz * seq_len, GLA_VW), BF16)
    if latent:
        cos, sin = rope
        in_specs += [pl.BlockSpec((seq_len, dk), lambda b, h: (0, 0)),
                     pl.BlockSpec((seq_len, dk), lambda b, h: (0, 0)),
                     pl.BlockSpec((None, None, 2, None, dk, dv), lambda b, h: (b, layer, 0, h, 0, 0))]
        args += [cos, sin, state]
        out_specs, out_shape, aliases = o_spec, o_shape, {}
    else:
        assert hps == GLA_HEADS
        in_specs += [pl.BlockSpec((seq_len, NA_WIDTH), lambda b, h: (b, COL_AK // NA_WIDTH)),
                     pl.BlockSpec((seq_len, NA_WIDTH), lambda b, h: (b, COL_AV // NA_WIDTH)),
                     pl.BlockSpec(memory_space=pl.ANY)]
        args += [p, p, after]
        aliases = {len(args) + i: 1 + i for i in range(len(state_prev))}
        in_specs += [pl.BlockSpec(memory_space=pl.ANY)] * len(state_prev)
        args += list(state_prev)
        if state_prev:
            s_spec = pl.BlockSpec((None, None, 2, hps, dk, dv), lambda b, h: (b, layer, 0, h, 0, 0))
            kv_spec = pl.BlockSpec((None, None, seq_len, NA_WIDTH), lambda b, h: (b, layer, 0, 0))
        else:
            s_spec = pl.BlockSpec((None, DEPTH, 2, hps, dk, dv), lambda b, h: (b, 0, 0, h, 0, 0))
            kv_spec = pl.BlockSpec((None, DEPTH, seq_len, NA_WIDTH), lambda b, h: (b, 0, 0, 0))
        kv_shape = jax.ShapeDtypeStruct((bsz, DEPTH, seq_len, NA_WIDTH), F32)
        out_specs = [o_spec, s_spec, kv_spec, kv_spec]
        out_shape = [o_shape, jax.ShapeDtypeStruct((bsz, DEPTH, 2, GLA_HEADS, dk, dv), F32), kv_shape, kv_shape]
    return pl.pallas_call(
        functools.partial(_gla_kernel, latent=latent, n_groups=seq_len // (GLA_GROUP * GLA_CHUNK), heads=hps,
                          layer=layer),
        grid=(bsz, GLA_HEADS // hps),
        in_specs=in_specs,
        out_specs=out_specs,
        out_shape=out_shape,
        input_output_aliases=aliases,
        scratch_shapes=[pltpu.VMEM((seq_len, kw), F32), pltpu.VMEM((seq_len, kw), F32),
                        pltpu.VMEM((seq_len, kw), F32), pltpu.VMEM((seq_len, kw), F32),
                        pltpu.VMEM((seq_len, vw), F32), pltpu.VMEM((seq_len, vw), F32),
                        pltpu.VMEM((vw, dk), F32), pltpu.VMEM((vw, dk), F32)],
        compiler_params=_cparams("parallel", "parallel"),
        name="gla_latent" if latent else "gla_context",
    )(*args)


def _conv_merge_kernel(u_ref, g_ref, up_ref, gp_ref, un_ref, gn_ref, z_ref, w_ref, b_ref, ng_ref, nb_ref,
                       oa_ref, ob_ref, ga_ref, gb_ref, gc_ref, wa_ref, wb_ref, wc_ref, o_ref, pad, cv,
                       *, tiles_per_seq, tt):
    m = jax.nn.sigmoid(ga_ref[...]) * _dot(oa_ref[...], wa_ref[...])
    m = m + jax.nn.sigmoid(gb_ref[...]) * _dot(ob_ref[...], wb_ref[...])
    t = pl.program_id(0) % tiles_per_seq
    halo = CONV_HALO
    glu = lambda u, g: u * jax.nn.sigmoid(g)
    pad[halo:halo + tt, :] = glu(u_ref[...], g_ref[...])
    pad[0:halo, :] = jnp.where(t > 0, glu(up_ref[...], gp_ref[...]), 0.0)
    pad[halo + tt:halo + tt + halo, :] = jnp.where(t < tiles_per_seq - 1, glu(un_ref[...], gn_ref[...]), 0.0)
    first = halo - CONV_TAPS // 2
    rb = 64
    reach = -(-(first + CONV_TAPS - 1) // SUBLANES) * SUBLANES
    for c0 in range(0, CONV_WIDTH, LANES):
        lanes = slice(c0, c0 + LANES)
        w_strip = w_ref[:, lanes]
        for r0 in range(0, tt, rb):
            base = pad[r0:r0 + rb + reach, lanes]
            acc = jnp.broadcast_to(b_ref[:, lanes], (rb, LANES))
            for phase in range(SUBLANES):
                rows = pltpu.roll(base, rb + reach - phase, 0) if phase else base
                for a in range(0, reach, SUBLANES):
                    j = a + phase - first
                    if 0 <= j < CONV_TAPS:
                        acc = acc + rows[a:a + rb] * w_strip[j:j + 1, :]
            cv[r0:r0 + rb, lanes] = acc
    acc = cv[...]
    mu = jnp.mean(acc, -1, keepdims=True)
    xc = acc - mu
    var = jnp.mean(xc * xc, -1, keepdims=True)
    y = xc * lax.rsqrt(var + LN_EPS) * ng_ref[...] + nb_ref[...]
    oc = (_silu(y) * _silu(z_ref[...])).astype(BF16)
    m = m + jax.nn.sigmoid(gc_ref[...]) * _dot(oc, wc_ref[...])
    o_ref[...] = m.astype(BF16)


def _conv_merge(p, og_a, og_b, conv_w, conv_b, norm_g, norm_b, wa, wb, wc, seq_len):
    t_tok = p.shape[0]
    tt = 256
    tiles_per_seq = seq_len // tt
    w = CONV_WIDTH
    d = D_MODEL
    hb = tt // CONV_HALO
    n_halo = t_tok // CONV_HALO
    main = lambda col: pl.BlockSpec((tt, w), lambda i: (i, col // w))
    prev = lambda col: pl.BlockSpec((CONV_HALO, w), lambda i: (jnp.maximum(i * hb - 1, 0), col // w))
    nxt = lambda col: pl.BlockSpec((CONV_HALO, w), lambda i: (jnp.minimum((i + 1) * hb, n_halo - 1), col // w))
    vec = pl.BlockSpec((1, w), lambda i: (0, 0))
    br = pl.BlockSpec((tt, w), lambda i: (i, 0))
    gate = lambda col: pl.BlockSpec((tt, d), lambda i: (i, col // d))
    wspec = pl.BlockSpec((w, d), lambda i: (0, 0), pipeline_mode=pl.Buffered(1))
    return pl.pallas_call(
        functools.partial(_conv_merge_kernel, tiles_per_seq=tiles_per_seq, tt=tt),
        grid=(t_tok // tt,),
        in_specs=[main(COL_CU), main(COL_CG), prev(COL_CU), prev(COL_CG), nxt(COL_CU), nxt(COL_CG), main(COL_CZ),
                  pl.BlockSpec((CONV_TAPS, w), lambda i: (0, 0)), vec, vec, vec,
                  br, br, gate(COL_GA), gate(COL_GB), gate(COL_GC), wspec, wspec, wspec],
        out_specs=pl.BlockSpec((tt, d), lambda i: (i, 0)),
        out_shape=jax.ShapeDtypeStruct((t_tok, d), BF16),
        scratch_shapes=[pltpu.VMEM((tt + 2 * CONV_HALO, w), F32), pltpu.VMEM((tt, w), F32)],
        compiler_params=_cparams("parallel"),
        name="conv_merge",
    )(p, p, p, p, p, p, p, conv_w, conv_b, norm_g, norm_b, og_a, og_b, p, p, p, wa, wb, wc)


OUT_SUB = 256


def _out_kernel(m_ref, w_ref, x_ref, mod_ref, g_ref, b_ref, o_ref):
    gate = mod_ref[0, :, 2 * D_MODEL:3 * D_MODEL]
    for r0 in range(0, m_ref.shape[0], OUT_SUB):
        rows = slice(r0, r0 + OUT_SUB)
        y = ALPHA * x_ref[rows, :] + gate * _dot(m_ref[rows, :], w_ref[...])
        mu = jnp.mean(y, -1, keepdims=True)
        yc = y - mu
        var = jnp.mean(yc * yc, -1, keepdims=True)
        o_ref[rows, :] = yc * lax.rsqrt(var + LN_EPS) * g_ref[...] + b_ref[...]


def _out_projection(merged, w_out, x2d, mod, ln_g, ln_b, seq_len):
    t_tok = x2d.shape[0]
    tm = 4 * OUT_SUB
    d = D_MODEL
    per = seq_len // tm if mod.shape[0] > 1 else t_tok // tm
    row = pl.BlockSpec((tm, d), lambda i: (i, 0))
    vec = pl.BlockSpec((1, d), lambda i: (0, 0))
    return pl.pallas_call(
        _out_kernel,
        grid=(t_tok // tm,),
        in_specs=[row, pl.BlockSpec((d, d), lambda i: (0, 0), pipeline_mode=pl.Buffered(1)), row,
                  pl.BlockSpec((1, 1, 3 * d), lambda i: (i // per, 0, 0)), vec, vec],
        out_specs=row,
        out_shape=jax.ShapeDtypeStruct((t_tok, d), F32),
        compiler_params=_cparams("parallel"),
        name="out_projection",
    )(merged, w_out, x2d, mod, ln_g, ln_b)


def kernel(x_prompt, x_sample, cache_k, cache_v, state_gla, c, c_ctx, w_mod, b_mod, w_in, rpb, gla_w_gate,
           gla_b_gate, gla_norm_g, conv_w, conv_b, conv_norm_g, conv_norm_b, w_proj_a, w_proj_b, w_proj_c,
           w_out, ln_g, ln_b):
    bsz, seq_len, d = x_prompt.shape
    dbsz, dseq, _ = x_sample.shape

    cvec = jnp.zeros((8, d), F32).at[0].set(c_ctx).at[1:1 + dbsz].set(c)
    mod_all = _modulation(cvec, w_mod, b_mod)

    w_in_t = jnp.swapaxes(w_in, 1, 2)
    w_t = _cast_rows_bf16(w_in_t)
    w_lr_t = jnp.pad(w_in_t[:, LR_START:LR_START + 2 * GLA_LOWRANK],
                     ((0, 0), (0, LANES - 2 * GLA_LOWRANK), (0, 0))).astype(BF16)
    wg = gla_w_gate.astype(BF16)
    wg_f = jnp.pad(wg[:, 0], ((0, 0), (0, LANES - GLA_LOWRANK), (0, 0)))
    wg_b = jnp.pad(wg[:, 1], ((0, 0), (GLA_LOWRANK, LANES - 2 * GLA_LOWRANK), (0, 0)))
    wpa, wpb, wpc, wo = (w.astype(BF16) for w in (w_proj_a, w_proj_b, w_proj_c, w_out))
    ck, cv = (a.reshape(a.shape[:3] + (NA_WIDTH,)) for a in (cache_k, cache_v))
    rope = _rope_tables(dseq)

    h_ctx = x_prompt.reshape(bsz * seq_len, d)
    h_lat = x_sample.reshape(dbsz * dseq, d)
    stacked = ()
    for l in range(DEPTH):
        mod_ctx = mod_all[l, 0:1][:, None, :]
        mod_lat = mod_all[l, 1:1 + dbsz][:, None, :]
        row = lambda a: a[l][None, :]
        gla_w = (wg_f[l], wg_b[l], gla_b_gate[l], row(gla_norm_g))
        conv_p = (conv_w[l], row(conv_b), row(conv_norm_g), row(conv_norm_b))

        def tail(x2d, p, og_a, og_b, mod, seq):
            merged = _conv_merge(p, og_a, og_b, *conv_p, wpa[l], wpb[l], wpc[l], seq)
            return _out_projection(merged, wo[l], x2d, mod, row(ln_g), row(ln_b), seq)

        p, p_lr = _in_projection(h_ctx, mod_ctx, w_t, w_lr_t, l, seq_len)
        og_a = _context_attention(p, bsz, seq_len)
        og_b, *stacked = _gla(p, p_lr, *gla_w, bsz, seq_len, l, state_prev=stacked, after=og_a)
        h_ctx = tail(h_ctx, p, og_a, og_b, mod_ctx, seq_len)

        p, p_lr = _in_projection(h_lat, mod_lat, w_t, w_lr_t, l, dseq)
        og_a = _neighbourhood_attention(p, ck, cv, rpb[l], l, dbsz, dseq)
        og_b = _gla(p, p_lr, *gla_w, dbsz, dseq, l, rope=rope, state=state_gla)
        h_lat = tail(h_lat, p, og_a, og_b, mod_lat, dseq)

    new_state, new_k, new_v = stacked
    new_k, new_v = (a.reshape(bsz, DEPTH, seq_len, NA_HEADS, NA_HEAD_DIM) for a in (new_k, new_v))
    return (h_ctx.reshape(bsz, seq_len, d), h_lat.reshape(dbsz, dseq, d), new_k, new_v, new_state)
```

```python
import functools

import numpy as np
import jax
import jax.numpy as jnp
from jax import lax
from jax.experimental import pallas as pl
from jax.experimental.pallas import tpu as pltpu

D_MODEL = 2048
DEPTH = 2
GRID_W = 64
NA_HEADS = 8
NA_HEAD_DIM = 128
NA_WIDTH = NA_HEADS * NA_HEAD_DIM
NA_WIN_H = 8
NA_WIN_W = 16
GLA_HEADS = 4
GLA_DK = 128
GLA_DV = 256
GLA_KW = GLA_HEADS * GLA_DK
GLA_VW = GLA_HEADS * GLA_DV
GLA_LOWRANK = 16
GLA_TAU = 16.0
GLA_CHUNK = 64
ROPE_BASE = 10000.0
CONV_WIDTH = 1024
CONV_TAPS = 31
ALPHA = (2 * DEPTH) ** 0.25
LN_EPS = 1e-5
NEG_INF = -1e30

LANES = 128
SUBLANES = 8
BF16_ROWS = 2 * SUBLANES
CONV_HALO = 16
VMEM_LIMIT = 60 * 1024 * 1024

COL_AQ, COL_AK, COL_AV, COL_AZ = 0, 1024, 2048, 3072
COL_BQ, COL_BK, COL_BV, COL_BZ = 4096, 4608, 5120, 6144
COL_CU, COL_CG, COL_CZ = 7168, 8192, 9216
COL_GA, COL_GB, COL_GC = 10240, 12288, 14336
N_MAIN = 16384
LR_START = 6144

BF16 = jnp.bfloat16
F32 = jnp.float32


def _cparams(*sem):
    return pltpu.CompilerParams(dimension_semantics=sem, vmem_limit_bytes=VMEM_LIMIT)


def _silu(x):
    return x * jax.nn.sigmoid(x)


def _dot(a, b):
    return jnp.dot(a, b, preferred_element_type=F32)


def _dot_nt(a, b):
    return lax.dot_general(a, b, (((1,), (1,)), ((), ())), preferred_element_type=F32)


def _mod_kernel(cv_ref, w_ref, b_ref, o_ref):
    s = _silu(cv_ref[...]).astype(BF16)
    o_ref[0] = _dot(s, w_ref[0].astype(BF16)) + b_ref[0]


def _modulation(cvec, w_mod, b_mod):
    tn = 1024
    n3 = 3 * D_MODEL
    return pl.pallas_call(
        _mod_kernel,
        grid=(DEPTH, n3 // tn),
        in_specs=[pl.BlockSpec((8, D_MODEL), lambda l, n: (0, 0)),
                  pl.BlockSpec((1, D_MODEL, tn), lambda l, n: (l, 0, n)),
                  pl.BlockSpec((1, 1, tn), lambda l, n: (l, 0, n))],
        out_specs=pl.BlockSpec((1, 8, tn), lambda l, n: (l, 0, n)),
        out_shape=jax.ShapeDtypeStruct((DEPTH, 8, n3), F32),
        compiler_params=_cparams("parallel", "parallel"),
        name="modulation",
    )(cvec, w_mod, b_mod.reshape(DEPTH, 1, n3))


def _cast_kernel(a_ref, o_ref):
    o_ref[...] = a_ref[...].astype(BF16)


def _cast_rows_bf16(w_t):
    depth, rows, d = w_t.shape
    tr = max(t for t in range(BF16_ROWS, 1024 + 1, BF16_ROWS) if rows % t == 0)
    spec = pl.BlockSpec((1, tr, d), lambda l, r: (l, r, 0))
    return pl.pallas_call(
        _cast_kernel,
        grid=(depth, rows // tr),
        in_specs=[spec],
        out_specs=spec,
        out_shape=jax.ShapeDtypeStruct(w_t.shape, BF16),
        compiler_params=_cparams("parallel", "parallel"),
        name="cast_w_in",
    )(w_t)


def _inproj_kernel(x_ref, mod_ref, w_ref, wlr_ref, o_ref, olr_ref, h_scr):
    strip = 256

    @pl.when(pl.program_id(1) == 0)
    def _():
        shift = mod_ref[0, :, 0:D_MODEL]
        scale = mod_ref[0, :, D_MODEL:2 * D_MODEL]

        def body(i, carry):
            sl = pl.ds(pl.multiple_of(i * strip, strip), strip)
            x = x_ref[sl, :]
            mu = jnp.mean(x, -1, keepdims=True)
            xc = x - mu
            var = jnp.mean(xc * xc, -1, keepdims=True)
            y = xc * lax.rsqrt(var + LN_EPS)
            hb = (y * (1.0 + scale) + shift).astype(BF16)
            h_scr[sl, :] = hb
            olr_ref[sl, :] = _dot_nt(hb, wlr_ref[...])
            return carry

        lax.fori_loop(0, x_ref.shape[0] // strip, body, 0)

    o_ref[...] = _dot_nt(h_scr[...], w_ref[0])


def _in_projection(x2d, mod, w_t, w_lr_t, layer, seq_len):
    t_tok = x2d.shape[0]
    tm, tn = 1024, 2048
    assert LR_START % tn == 0
    per = seq_len // tm if mod.shape[0] > 1 else t_tok // tm
    w_row = lambda n: pl.multiple_of(n * tn + jnp.where(n >= LR_START // tn, 2 * GLA_LOWRANK, 0), 2 * GLA_LOWRANK)
    return pl.pallas_call(
        _inproj_kernel,
        grid=(t_tok // tm, N_MAIN // tn),
        in_specs=[pl.BlockSpec((tm, D_MODEL), lambda m, n: (m, 0)),
                  pl.BlockSpec((1, 1, 3 * D_MODEL), lambda m, n: (m // per, 0, 0)),
                  pl.BlockSpec((pl.Element(1), pl.Element(tn), pl.Element(D_MODEL)),
                               lambda m, n: (layer, w_row(n), 0)),
                  pl.BlockSpec((None, LANES, D_MODEL), lambda m, n: (layer, 0, 0))],
        out_specs=[pl.BlockSpec((tm, tn), lambda m, n: (m, n)),
                   pl.BlockSpec((tm, LANES), lambda m, n: (m, 0))],
        out_shape=[jax.ShapeDtypeStruct((t_tok, N_MAIN), F32),
                   jax.ShapeDtypeStruct((t_tok, LANES), F32)],
        scratch_shapes=[pltpu.VMEM((tm, D_MODEL), BF16)],
        compiler_params=_cparams("parallel", "arbitrary"),
        name="in_projection",
    )(x2d, mod, w_t, w_lr_t)


def _ctx_attn_kernel(q_ref, k_ref, v_ref, z_ref, o_ref, *, seq_len):
    scale = NA_HEAD_DIM ** -0.5
    for r0 in range(0, q_ref.shape[0], seq_len):
        rows = slice(r0, r0 + seq_len)
        for h in range(NA_HEADS):
            cs = slice(h * NA_HEAD_DIM, (h + 1) * NA_HEAD_DIM)
            q = q_ref[rows, cs].astype(BF16)
            k = k_ref[rows, cs].astype(BF16)
            v = v_ref[rows, cs].astype(BF16)
            s = _dot_nt(q, k) * scale
            e = jnp.exp(s - jnp.max(s, -1, keepdims=True))
            p = e / jnp.sum(e, -1, keepdims=True)
            o = _dot(p.astype(BF16), v)
            o_ref[rows, cs] = (o * _silu(z_ref[rows, cs])).astype(BF16)


def _context_attention(p_ctx, bsz, seq_len):
    per_step = 2 if bsz % 2 == 0 else 1
    tr = per_step * seq_len
    blk = lambda j: pl.BlockSpec((tr, NA_WIDTH), lambda b: (b, j))
    return pl.pallas_call(
        functools.partial(_ctx_attn_kernel, seq_len=seq_len),
        grid=(bsz // per_step,),
        in_specs=[blk(COL_AQ // NA_WIDTH), blk(COL_AK // NA_WIDTH), blk(COL_AV // NA_WIDTH), blk(COL_AZ // NA_WIDTH)],
        out_specs=pl.BlockSpec((tr, NA_WIDTH), lambda b: (b, 0)),
        out_shape=jax.ShapeDtypeStruct((bsz * seq_len, NA_WIDTH), BF16),
        compiler_params=_cparams("parallel"),
        name="context_attention",
    )(p_ctx, p_ctx, p_ctx, p_ctx)


NA_QROWS = 4
NA_KROWS = NA_QROWS + NA_WIN_H


NA_NDR = 2 * NA_WIN_H


def _na_plan(rows):
    kh = min(NA_WIN_H, rows)
    masked = NA_NDR - 1
    groups = []
    for r0 in range(0, rows, NA_QROWS):
        ks = int(np.clip(r0 - kh // 2, 0, rows - NA_KROWS))
        tiles = []
        for rq in range(NA_QROWS):
            r = r0 + rq
            rs = int(np.clip(r - kh // 2, 0, rows - kh))
            slot = [ks + i - r + NA_WIN_H - 1 if rs <= ks + i < rs + kh else masked for i in range(NA_KROWS)]
            row_tiles = []
            for i in range(0, NA_KROWS, 2):
                a, b = slot[i], slot[i + 1]
                if a != masked and b != masked:
                    row_tiles.append(b)
                elif b != masked:
                    row_tiles.append(NA_NDR + b)
                else:
                    row_tiles.append(2 * NA_NDR + a)
            tiles.append(tuple(row_tiles))
        groups.append((r0, ks, tuple(tiles)))
    return tuple(groups)


def _na_bias_tiles(rpb_l):
    nh = rpb_l.shape[0]
    n_dr = 2 * NA_WIN_H - 1
    c = np.arange(GRID_W)
    cs = np.clip(c - NA_WIN_W // 2, 0, GRID_W - NA_WIN_W)
    valid = (c[None, :] >= cs[:, None]) & (c[None, :] < cs[:, None] + NA_WIN_W)
    span = 2 * GRID_W
    left = (GRID_W - 1) - (NA_WIN_W - 1)
    wv = jnp.pad(rpb_l.astype(F32), ((0, 0), (0, 0), (left, span - left - (2 * NA_WIN_W - 1))))
    skew = jnp.tile(wv, (1, 1, GRID_W))[..., :GRID_W * (span - 1)].reshape(nh, n_dr, GRID_W, span - 1)
    toe = jnp.where(valid, skew[..., GRID_W - 1:], NEG_INF)
    neg1 = jnp.full((nh, 1, GRID_W, GRID_W), NEG_INF, F32)
    negs = jnp.full((nh, NA_NDR, GRID_W, GRID_W), NEG_INF, F32)
    cur = jnp.concatenate([toe, neg1], axis=1)
    prev = jnp.concatenate([neg1, toe], axis=1)
    return jnp.concatenate([jnp.concatenate([prev, cur], -1), jnp.concatenate([negs, cur], -1),
                            jnp.concatenate([cur, negs], -1)], axis=1)


def _na_kernel(q_ref, k_ref, v_ref, z_ref, kc_ref, vc_ref, bias_ref, o_ref, kb_scr, vb_scr, *, groups):
    scale = NA_HEAD_DIM ** -0.5
    kb_scr[...] = k_ref[...].astype(BF16)
    vb_scr[...] = v_ref[...].astype(BF16)
    kctx = kc_ref[...].astype(BF16)
    vctx = vc_ref[...].astype(BF16)
    nq, nk = NA_QROWS * GRID_W, NA_KROWS * GRID_W
    for r0, ks, tiles in groups:
        qsl = slice(r0 * GRID_W, r0 * GRID_W + nq)
        ksl = slice(ks * GRID_W, ks * GRID_W + nk)
        q = q_ref[qsl, :].astype(BF16)
        bias = jnp.concatenate([jnp.concatenate([bias_ref[t] for t in row], axis=1) for row in tiles], axis=0)
        s_win = _dot_nt(q, kb_scr[ksl, :]) * scale + bias
        s_ctx = _dot_nt(q, kctx) * scale
        m = jnp.maximum(jnp.max(s_win, -1, keepdims=True), jnp.max(s_ctx, -1, keepdims=True))
        e_win = jnp.exp(s_win - m)
        e_ctx = jnp.exp(s_ctx - m)
        den = jnp.sum(e_win, -1, keepdims=True) + jnp.sum(e_ctx, -1, keepdims=True)
        o = _dot((e_win / den).astype(BF16), vb_scr[ksl, :]) + _dot((e_ctx / den).astype(BF16), vctx)
        o_ref[qsl, :] = (o * _silu(z_ref[qsl, :])).astype(BF16)


def _neighbourhood_attention(p_lat, cache_k, cache_v, rpb_l, layer, bsz, n_tok):
    rows = n_tok // GRID_W
    assert rows % NA_QROWS == 0 and rows >= NA_KROWS and NA_KROWS % 2 == 0
    groups = _na_plan(rows)
    bias_tab = _na_bias_tiles(rpb_l)
    past = cache_k.shape[2]
    hd = NA_HEAD_DIM
    blk = lambda col: pl.BlockSpec((n_tok, hd), lambda b, h: (b, col // hd + h))
    cblk = pl.BlockSpec((None, None, past, hd), lambda b, h: (b, layer, 0, h))
    return pl.pallas_call(
        functools.partial(_na_kernel, groups=groups),
        grid=(bsz, NA_HEADS),
        in_specs=[blk(COL_AQ), blk(COL_AK), blk(COL_AV), blk(COL_AZ), cblk, cblk,
                  pl.BlockSpec((None,) + bias_tab.shape[1:], lambda b, h: (h, 0, 0, 0))],
        out_specs=pl.BlockSpec((n_tok, hd), lambda b, h: (b, h)),
        out_shape=jax.ShapeDtypeStruct((bsz * n_tok, NA_WIDTH), BF16),
        scratch_shapes=[pltpu.VMEM((n_tok, hd), BF16), pltpu.VMEM((n_tok, hd), BF16)],
        compiler_params=_cparams("parallel", "parallel"),
        name="neighbourhood_attention",
    )(p_lat, p_lat, p_lat, p_lat, cache_k, cache_v, bias_tab)


def _rope_tables(seq_len):
    t = jnp.arange(seq_len)
    half = GLA_DK // 2
    inv = ROPE_BASE ** (-jnp.arange(0, half, 2, dtype=F32) / half)

    def tab(pos):
        ang = pos.astype(F32)[:, None] * inv[None, :]
        return jnp.cos(ang), jnp.sin(ang)

    cr, sr = tab(t // GRID_W)
    cc, sc = tab(t % GRID_W)
    cos = jnp.concatenate([cr, cr, cc, cc], -1)
    sin = jnp.concatenate([-sr, sr, -sc, sc], -1)
    return cos, sin


def _rope(x, cos, sin):
    quarter = GLA_DK // 4
    lane = lax.broadcasted_iota(jnp.int32, x.shape, 1)
    partner = jnp.where((lane % (2 * quarter)) < quarter,
                        pltpu.roll(x, GLA_DK - quarter, 1), pltpu.roll(x, quarter, 1))
    return x * cos + partner * sin


def _log_sigmoid(x):
    return jnp.minimum(x, 0.0) - jnp.log1p(jnp.exp(-jnp.abs(x)))


def _chunk_scan(x, reverse):
    n = x.shape[0]
    pos = lax.broadcasted_iota(jnp.int32, x.shape, 0) % GLA_CHUNK
    step = 1
    while step < GLA_CHUNK:
        if reverse:
            x = x + jnp.where(pos < GLA_CHUNK - step, pltpu.roll(x, n - step, 0), 0.0)
        else:
            x = x + jnp.where(pos >= step, pltpu.roll(x, step, 0), 0.0)
        step *= 2
    return x


GLA_GROUP = 4


def _gla_kernel(*refs, latent, n_groups, heads, layer):
    (q_ref, k_ref, v_ref, z_ref, lr_ref, wgf_ref, wgb_ref, bg_ref, ng_ref), rest = refs[:9], refs[9:]
    if latent:
        cos_ref, sin_ref, s0_ref, o_ref, qs, ks, laf, lab, o_f, o_b, st_f, st_b = rest
    else:
        ak_ref, av_ref = rest[:2]
        o_ref, sfin_ref, ko_ref, vo_ref, qs, ks, laf, lab, o_f, o_b, st_f, st_b = rest[-12:]
        if len(rest) == 15:
            for l in range(DEPTH):
                if l != layer:
                    sfin_ref[l] = jnp.zeros(sfin_ref.shape[1:], F32)
                    ko_ref[l] = jnp.zeros(ko_ref.shape[1:], F32)
                    vo_ref[l] = jnp.zeros(vo_ref.shape[1:], F32)
            sfin_ref, ko_ref, vo_ref = sfin_ref.at[layer], ko_ref.at[layer], vo_ref.at[layer]
        ko_ref[...] = ak_ref[...]
        vo_ref[...] = av_ref[...]
    c_len = GLA_CHUNK
    g_len = GLA_GROUP * c_len
    dk, dv = GLA_DK, GLA_DV

    q = q_ref[...] * (GLA_DK ** -0.5)
    k = k_ref[...]
    if latent:
        q = _rope(q, cos_ref[...], sin_ref[...])
        k = _rope(k, cos_ref[...], sin_ref[...])
    qs[...] = q
    ks[...] = k
    lrb = lr_ref[...].astype(BF16)
    laf[...] = _log_sigmoid(_dot(lrb, wgf_ref[...]) + bg_ref[0:1, :]) * (1.0 / GLA_TAU)
    lab[...] = _log_sigmoid(_dot(lrb, wgb_ref[...]) + bg_ref[1:2, :]) * (1.0 / GLA_TAU)

    ii = lax.broadcasted_iota(jnp.int32, (c_len, c_len), 0)
    jj = lax.broadcasted_iota(jnp.int32, (c_len, c_len), 1)

    def half_group(g, reverse, hh):
        la_ref, out, st = (lab, o_b, st_b) if reverse else (laf, o_f, st_f)
        kcols, vcols, srows = slice(hh * dk, (hh + 1) * dk), slice(hh * dv, (hh + 1) * dv), slice(hh * dv, (hh + 1) * dv)
        mask = (ii <= jj) if reverse else (ii >= jj)
        edge = 0 if reverse else c_len - 1
        start = g * g_len if isinstance(g, int) else pl.multiple_of(g * g_len, g_len)
        b_all = _chunk_scan(la_ref[pl.ds(start, g_len), kcols], reverse)
        s_t = st[srows, :]
        for c in (range(GLA_GROUP - 1, -1, -1) if reverse else range(GLA_GROUP)):
            sl = pl.ds(start + c * c_len, c_len)
            b = b_all[c * c_len:(c + 1) * c_len]
            btot = b[edge:edge + 1, :]
            qc = qs[sl, kcols]
            kc = ks[sl, kcols]
            vc = v_ref[sl, vcols]
            qe = (qc * jnp.exp(b)).astype(BF16)
            ke = (kc * jnp.exp(-b)).astype(BF16)
            kw = (kc * jnp.exp(btot - b)).astype(BF16)
            att = jnp.where(mask, _dot_nt(qe, ke), 0.0)
            out[sl, vcols] = _dot(att.astype(BF16), vc.astype(BF16)) + _dot_nt(qe, s_t.astype(BF16))
            s_t = jnp.exp(btot) * s_t + _dot(vc.T.astype(BF16), kw)
        st[srows, :] = s_t

    for direction, st in enumerate((st_f, st_b)):
        st[...] = s0_ref[direction].T if latent else jnp.zeros_like(st)
    if n_groups == 1:
        for hh in range(heads):
            half_group(0, False, hh)
            half_group(0, True, hh)
    else:
        def body(i, carry):
            for hh in range(heads):
                half_group(i, False, hh)
                half_group(n_groups - 1 - i, True, hh)
            return carry
        lax.fori_loop(0, n_groups, body, 0)

    for hh in range(heads):
        kcols, vcols = slice(hh * dk, (hh + 1) * dk), slice(hh * dv, (hh + 1) * dv)
        if not latent:
            sfin_ref[0, hh] = st_f[vcols, :].T
            sfin_ref[1, hh] = st_b[vcols, :].T
        o = (o_f[:, vcols] + o_b[:, vcols]) - jnp.sum(qs[:, kcols] * ks[:, kcols], -1, keepdims=True) * v_ref[:, vcols]
        o = o * lax.rsqrt(jnp.mean(o * o, -1, keepdims=True) + LN_EPS) * ng_ref[:, vcols]
        o_ref[:, vcols] = (o * _silu(z_ref[:, vcols])).astype(BF16)


def _gla(p, p_lr, wg_f, wg_b, b_gate, norm_g, bsz, seq_len, layer, rope=None, state=None, state_prev=(), after=None):
    latent = state is not None
    assert seq_len % (GLA_GROUP * GLA_CHUNK) == 0
    dk, dv = GLA_DK, GLA_DV
    hps = 1 if latent else GLA_HEADS
    kw, vw = hps * dk, hps * dv
    in_specs = [pl.BlockSpec((seq_len, kw), lambda b, h: (b, COL_BQ // kw + h)),
                pl.BlockSpec((seq_len, kw), lambda b, h: (b, COL_BK // kw + h)),
                pl.BlockSpec((seq_len, vw), lambda b, h: (b, COL_BV // vw + h)),
                pl.BlockSpec((seq_len, vw), lambda b, h: (b, COL_BZ // vw + h)),
                pl.BlockSpec((seq_len, LANES), lambda b, h: (b, 0)),
                pl.BlockSpec((LANES, kw), lambda b, h: (0, h)),
                pl.BlockSpec((LANES, kw), lambda b, h: (0, h)),
                pl.BlockSpec((2, kw), lambda b, h: (0, h)),
                pl.BlockSpec((1, vw), lambda b, h: (0, h))]
    args = [p, p, p, p, p_lr, wg_f, wg_b, b_gate, norm_g]
    o_spec = pl.BlockSpec((seq_len, vw), lambda b, h: (b, h))
    o_shape = jax.ShapeDtypeStruct((bsz * seq_len, GLA_VW), BF16)
    if latent:
        cos, sin = rope
        in_specs += [pl.BlockSpec((seq_len, dk), lambda b, h: (0, 0)),
                     pl.BlockSpec((seq_len, dk), lambda b, h: (0, 0)),
                     pl.BlockSpec((None, None, 2, None, dk, dv), lambda b, h: (b, layer, 0, h, 0, 0))]
        args += [cos, sin, state]
        out_specs, out_shape, aliases = o_spec, o_shape, {}
    else:
        assert hps == GLA_HEADS
        in_specs += [pl.BlockSpec((seq_len, NA_WIDTH), lambda b, h: (b, COL_AK // NA_WIDTH)),
                     pl.BlockSpec((seq_len, NA_WIDTH), lambda b, h: (b, COL_AV // NA_WIDTH)),
                     pl.BlockSpec(memory_space=pl.ANY)]
        args += [p, p, after]
        aliases = {len(args) + i: 1 + i for i in range(len(state_prev))}
        in_specs += [pl.BlockSpec(memory_space=pl.ANY)] * len(state_prev)
        args += list(state_prev)
        if state_prev:
            s_spec = pl.BlockSpec((None, None, 2, hps, dk, dv), lambda b, h: (b, layer, 0, h, 0, 0))
            kv_spec = pl.BlockSpec((None, None, seq_len, NA_WIDTH), lambda b, h: (b, layer, 0, 0))
        else:
            s_spec = pl.BlockSpec((None, DEPTH, 2, hps, dk, dv), lambda b, h: (b, 0, 0, h, 0, 0))
            kv_spec = pl.BlockSpec((None, DEPTH, seq_len, NA_WIDTH), lambda b, h: (b, 0, 0, 0))
        kv_shape = jax.ShapeDtypeStruct((bsz, DEPTH, seq_len, NA_WIDTH), F32)
        out_specs = [o_spec, s_spec, kv_spec, kv_spec]
        out_shape = [o_shape, jax.ShapeDtypeStruct((bsz, DEPTH, 2, GLA_HEADS, dk, dv), F32), kv_shape, kv_shape]
    return pl.pallas_call(
        functools.partial(_gla_kernel, latent=latent, n_groups=seq_len // (GLA_GROUP * GLA_CHUNK), heads=hps,
                          layer=layer),
        grid=(bsz, GLA_HEADS // hps),
        in_specs=in_specs,
        out_specs=out_specs,
        out_shape=out_shape,
        input_output_aliases=aliases,
        scratch_shapes=[pltpu.VMEM((seq_len, kw), F32), pltpu.VMEM((seq_len, kw), F32),
                        pltpu.VMEM((seq_len, kw), F32), pltpu.VMEM((seq_len, kw), F32),
                        pltpu.VMEM((seq_len, vw), F32), pltpu.VMEM((seq_len, vw), F32),
                        pltpu.VMEM((vw, dk), F32), pltpu.VMEM((vw, dk), F32)],
        compiler_params=_cparams("parallel", "parallel"),
        name="gla_latent" if latent else "gla_context",
    )(*args)


def _conv_merge_kernel(u_ref, g_ref, up_ref, gp_ref, un_ref, gn_ref, z_ref, w_ref, b_ref, ng_ref, nb_ref,
                       oa_ref, ob_ref, ga_ref, gb_ref, gc_ref, wa_ref, wb_ref, wc_ref, o_ref, pad, cv,
                       *, tiles_per_seq, tt):
    t = pl.program_id(0) % tiles_per_seq
    halo = CONV_HALO
    glu = lambda u, g: u * jax.nn.sigmoid(g)
    pad[halo:halo + tt, :] = glu(u_ref[...], g_ref[...])
    pad[0:halo, :] = jnp.where(t > 0, glu(up_ref[...], gp_ref[...]), 0.0)
    pad[halo + tt:halo + tt + halo, :] = jnp.where(t < tiles_per_seq - 1, glu(un_ref[...], gn_ref[...]), 0.0)
    first = halo - CONV_TAPS // 2
    rb = 64
    reach = -(-(first + CONV_TAPS - 1) // SUBLANES) * SUBLANES
    for c0 in range(0, CONV_WIDTH, LANES):
        lanes = slice(c0, c0 + LANES)
        w_strip = w_ref[:, lanes]
        for r0 in range(0, tt, rb):
            base = pad[r0:r0 + rb + reach, lanes]
            acc = jnp.broadcast_to(b_ref[:, lanes], (rb, LANES))
            for phase in range(SUBLANES):
                rows = pltpu.roll(base, rb + reach - phase, 0) if phase else base
                for a in range(0, reach, SUBLANES):
                    j = a + phase - first
                    if 0 <= j < CONV_TAPS:
                        acc = acc + rows[a:a + rb] * w_strip[j:j + 1, :]
            cv[r0:r0 + rb, lanes] = acc
    acc = cv[...]
    mu = jnp.mean(acc, -1, keepdims=True)
    xc = acc - mu
    var = jnp.mean(xc * xc, -1, keepdims=True)
    y = xc * lax.rsqrt(var + LN_EPS) * ng_ref[...] + nb_ref[...]
    oc = (_silu(y) * _silu(z_ref[...])).astype(BF16)
    mc = D_MODEL // 4
    for c0 in range(0, D_MODEL, mc):
        cols = slice(c0, c0 + mc)
        m = jax.nn.sigmoid(ga_ref[:, cols]) * _dot(oa_ref[...], wa_ref[:, cols])
        m = m + jax.nn.sigmoid(gb_ref[:, cols]) * _dot(ob_ref[...], wb_ref[:, cols])
        m = m + jax.nn.sigmoid(gc_ref[:, cols]) * _dot(oc, wc_ref[:, cols])
        o_ref[:, cols] = m.astype(BF16)


def _conv_merge(p, og_a, og_b, conv_w, conv_b, norm_g, norm_b, wa, wb, wc, seq_len):
    t_tok = p.shape[0]
    tt = 256
    tiles_per_seq = seq_len // tt
    w = CONV_WIDTH
    d = D_MODEL
    hb = tt // CONV_HALO
    n_halo = t_tok // CONV_HALO
    main = lambda col: pl.BlockSpec((tt, w), lambda i: (i, col // w))
    prev = lambda col: pl.BlockSpec((CONV_HALO, w), lambda i: (jnp.maximum(i * hb - 1, 0), col // w))
    nxt = lambda col: pl.BlockSpec((CONV_HALO, w), lambda i: (jnp.minimum((i + 1) * hb, n_halo - 1), col // w))
    vec = pl.BlockSpec((1, w), lambda i: (0, 0))
    br = pl.BlockSpec((tt, w), lambda i: (i, 0))
    gate = lambda col: pl.BlockSpec((tt, d), lambda i: (i, col // d))
    wspec = pl.BlockSpec((w, d), lambda i: (0, 0), pipeline_mode=pl.Buffered(1))
    return pl.pallas_call(
        functools.partial(_conv_merge_kernel, tiles_per_seq=tiles_per_seq, tt=tt),
        grid=(t_tok // tt,),
        in_specs=[main(COL_CU), main(COL_CG), prev(COL_CU), prev(COL_CG), nxt(COL_CU), nxt(COL_CG), main(COL_CZ),
                  pl.BlockSpec((CONV_TAPS, w), lambda i: (0, 0)), vec, vec, vec,
                  br, br, gate(COL_GA), gate(COL_GB), gate(COL_GC), wspec, wspec, wspec],
        out_specs=pl.BlockSpec((tt, d), lambda i: (i, 0)),
        out_shape=jax.ShapeDtypeStruct((t_tok, d), BF16),
        scratch_shapes=[pltpu.VMEM((tt + 2 * CONV_HALO, w), F32), pltpu.VMEM((tt, w), F32)],
        compiler_params=_cparams("parallel"),
        name="conv_merge",
    )(p, p, p, p, p, p, p, conv_w, conv_b, norm_g, norm_b, og_a, og_b, p, p, p, wa, wb, wc)


OUT_SUB = 256


def _out_kernel(m_ref, w_ref, x_ref, mod_ref, g_ref, b_ref, o_ref):
    gate = mod_ref[0, :, 2 * D_MODEL:3 * D_MODEL]
    for r0 in range(0, m_ref.shape[0], OUT_SUB):
        rows = slice(r0, r0 + OUT_SUB)
        y = ALPHA * x_ref[rows, :] + gate * _dot(m_ref[rows, :], w_ref[...])
        mu = jnp.mean(y, -1, keepdims=True)
        yc = y - mu
        var = jnp.mean(yc * yc, -1, keepdims=True)
        o_ref[rows, :] = yc * lax.rsqrt(var + LN_EPS) * g_ref[...] + b_ref[...]


def _out_projection(merged, w_out, x2d, mod, ln_g, ln_b, seq_len):
    t_tok = x2d.shape[0]
    tm = 4 * OUT_SUB
    d = D_MODEL
    per = seq_len // tm if mod.shape[0] > 1 else t_tok // tm
    row = pl.BlockSpec((tm, d), lambda i: (i, 0))
    vec = pl.BlockSpec((1, d), lambda i: (0, 0))
    return pl.pallas_call(
        _out_kernel,
        grid=(t_tok // tm,),
        in_specs=[row, pl.BlockSpec((d, d), lambda i: (0, 0), pipeline_mode=pl.Buffered(1)), row,
                  pl.BlockSpec((1, 1, 3 * d), lambda i: (i // per, 0, 0)), vec, vec],
        out_specs=row,
        out_shape=jax.ShapeDtypeStruct((t_tok, d), F32),
        compiler_params=_cparams("parallel"),
        name="out_projection",
    )(merged, w_out, x2d, mod, ln_g, ln_b)


def kernel(x_prompt, x_sample, cache_k, cache_v, state_gla, c, c_ctx, w_mod, b_mod, w_in, rpb, gla_w_gate,
           gla_b_gate, gla_norm_g, conv_w, conv_b, conv_norm_g, conv_norm_b, w_proj_a, w_proj_b, w_proj_c,
           w_out, ln_g, ln_b):
    bsz, seq_len, d = x_prompt.shape
    dbsz, dseq, _ = x_sample.shape

    cvec = jnp.zeros((8, d), F32).at[0].set(c_ctx).at[1:1 + dbsz].set(c)
    mod_all = _modulation(cvec, w_mod, b_mod)

    w_in_t = jnp.swapaxes(w_in, 1, 2)
    w_t = _cast_rows_bf16(w_in_t)
    w_lr_t = jnp.pad(w_in_t[:, LR_START:LR_START + 2 * GLA_LOWRANK],
                     ((0, 0), (0, LANES - 2 * GLA_LOWRANK), (0, 0))).astype(BF16)
    wg = gla_w_gate.astype(BF16)
    wg_f = jnp.pad(wg[:, 0], ((0, 0), (0, LANES - GLA_LOWRANK), (0, 0)))
    wg_b = jnp.pad(wg[:, 1], ((0, 0), (GLA_LOWRANK, LANES - 2 * GLA_LOWRANK), (0, 0)))
    wpa, wpb, wpc, wo = (w.astype(BF16) for w in (w_proj_a, w_proj_b, w_proj_c, w_out))
    ck, cv = (a.reshape(a.shape[:3] + (NA_WIDTH,)) for a in (cache_k, cache_v))
    rope = _rope_tables(dseq)

    h_ctx = x_prompt.reshape(bsz * seq_len, d)
    h_lat = x_sample.reshape(dbsz * dseq, d)
    stacked = ()
    for l in range(DEPTH):
        mod_ctx = mod_all[l, 0:1][:, None, :]
        mod_lat = mod_all[l, 1:1 + dbsz][:, None, :]
        row = lambda a: a[l][None, :]
        gla_w = (wg_f[l], wg_b[l], gla_b_gate[l], row(gla_norm_g))
        conv_p = (conv_w[l], row(conv_b), row(conv_norm_g), row(conv_norm_b))

        def tail(x2d, p, og_a, og_b, mod, seq):
            merged = _conv_merge(p, og_a, og_b, *conv_p, wpa[l], wpb[l], wpc[l], seq)
            return _out_projection(merged, wo[l], x2d, mod, row(ln_g), row(ln_b), seq)

        p, p_lr = _in_projection(h_ctx, mod_ctx, w_t, w_lr_t, l, seq_len)
        og_a = _context_attention(p, bsz, seq_len)
        og_b, *stacked = _gla(p, p_lr, *gla_w, bsz, seq_len, l, state_prev=stacked, after=og_a)
        h_ctx = tail(h_ctx, p, og_a, og_b, mod_ctx, seq_len)

        p, p_lr = _in_projection(h_lat, mod_lat, w_t, w_lr_t, l, dseq)
        og_a = _neighbourhood_attention(p, ck, cv, rpb[l], l, dbsz, dseq)
        og_b = _gla(p, p_lr, *gla_w, dbsz, dseq, l, rope=rope, state=state_gla)
        h_lat = tail(h_lat, p, og_a, og_b, mod_lat, dseq)

    new_state, new_k, new_v = stacked
    new_k, new_v = (a.reshape(bsz, DEPTH, seq_len, NA_HEADS, NA_HEAD_DIM) for a in (new_k, new_v))
    return (h_ctx.reshape(bsz, seq_len, d), h_lat.reshape(dbsz, dseq, d), new_k, new_v, new_state)
```
